```python
import math
import jax, jax.numpy as jnp
from jax import lax
import numpy as np

D_MODEL = 2048
BATCH = 2
SEQ = 4096
DEPTH = 4
DEC_BATCH = 32
DEC_SEQ = 16
PAST_LEN = 2048

CHUNK = 64
Q_BLOCK = 128
N_MIXERS = 2
N_MLA = (DEPTH + 1) // 2
N_DSA = DEPTH // 2
MLA_HEADS = 16
MLA_Q_LORA = 1536
MLA_KV_LORA = 512
MLA_NOPE = 128
MLA_ROPE = 64
MLA_V = 128
ROPE_THETA = 10000.0
DSA_HEADS = 16
DSA_KV_HEADS = 4
DSA_GROUP = DSA_HEADS // DSA_KV_HEADS
DSA_HEAD_DIM = 128
IDX_HEADS = 16
IDX_DIM = 64
TOPK_MAX = 256
DSA_Q_COLS = DSA_HEADS * DSA_HEAD_DIM
DSA_KV_COLS = DSA_KV_HEADS * DSA_HEAD_DIM
DSA_IQ_COLS = IDX_HEADS * IDX_DIM
DSA_SPLITS = (DSA_Q_COLS, DSA_Q_COLS + DSA_KV_COLS, DSA_Q_COLS + 2 * DSA_KV_COLS,
              DSA_Q_COLS + 2 * DSA_KV_COLS + DSA_IQ_COLS, DSA_Q_COLS + 2 * DSA_KV_COLS + DSA_IQ_COLS + IDX_DIM)
DSA_IN_COLS = DSA_SPLITS[-1] + IDX_HEADS
REL_BUCKETS = 32
REL_MAX_DIST = 128
D_FF = 5632
NORM_EPS = 1e-6

kernel_name = 'streaming_mla_dsa_macaron_step'


def rms_norm(x, g):
    xf = x.astype(jnp.float32)
    y = xf * lax.rsqrt(jnp.mean(xf * xf, axis=-1, keepdims=True) + NORM_EPS)
    return (y * g.astype(jnp.float32)).astype(x.dtype)


def apply_rope(x, pos):
    half = x.shape[-1] // 2
    inv_freq = ROPE_THETA ** (-jnp.arange(half, dtype=jnp.float32) / half)
    ang = pos.astype(jnp.float32)[:, None] * inv_freq[None, :]
    if x.ndim == 4:
        ang = ang[:, None, :]
    cos, sin = jnp.cos(ang), jnp.sin(ang)
    xf = x.astype(jnp.float32)
    x1, x2 = xf[..., :half], xf[..., half:]
    return jnp.concatenate([x1 * cos - x2 * sin, x1 * sin + x2 * cos], axis=-1).astype(x.dtype)


def chunk_mask(q_pos, k_pos):
    return (k_pos // CHUNK)[None, :] <= (q_pos // CHUNK)[:, None]


def t5_bucket(dist):
    half = REL_BUCKETS // 2
    max_exact = half // 2
    side = jnp.where(dist < 0, half, 0)
    a = jnp.abs(dist)
    a_f = jnp.maximum(a, 1).astype(jnp.float32)
    large = max_exact + (jnp.log(a_f / max_exact) / math.log(REL_MAX_DIST / max_exact)
                         * (half - max_exact)).astype(jnp.int32)
    large = jnp.minimum(large, half - 1)
    return side + jnp.where(a < max_exact, a, large)


def map_query_blocks(fn, q_pos, *q_args):
    T = q_pos.shape[0]
    if T <= Q_BLOCK:
        return fn(q_pos, *q_args)
    nb = T // Q_BLOCK

    def split(a):
        return jnp.swapaxes(a.reshape(a.shape[0], nb, Q_BLOCK, *a.shape[2:]), 0, 1)

    out = lax.map(lambda args: fn(*args),
                  (q_pos.reshape(nb, Q_BLOCK),) + tuple(split(a) for a in q_args))
    out = jnp.swapaxes(out, 0, 1)
    return out.reshape(out.shape[0], T, *out.shape[3:])


def swiglu(h, w_in, w_out):
    gate, up = jnp.split(h @ w_in, 2, axis=-1)
    return (jax.nn.silu(gate) * up) @ w_out


def mla_mixer(h, pos, past_c, past_kr, w_down, g_q, g_kv, w_uq, w_uk, w_uv, w_o):
    B, T, _ = h.shape
    cq, ckv, kr = jnp.split(h @ w_down, [MLA_Q_LORA, MLA_Q_LORA + MLA_KV_LORA], axis=-1)
    cq = rms_norm(cq, g_q)
    ckv = rms_norm(ckv, g_kv)
    kr = apply_rope(kr, pos)
    q = jnp.einsum('btc,chn->bthn', cq, w_uq)
    q_nope = q[..., :MLA_NOPE]
    q_rope = apply_rope(q[..., MLA_NOPE:], pos)
    q_lat = jnp.einsum('bthn,chn->bthc', q_nope, w_uk)
    if past_c is None:
        c_all, kr_all = ckv, kr
    else:
        c_all = jnp.concatenate([past_c, ckv], axis=1)
        kr_all = jnp.concatenate([past_kr, kr], axis=1)
    k_pos = jnp.arange(c_all.shape[1], dtype=jnp.int32)
    scale = (MLA_NOPE + MLA_ROPE) ** -0.5

    def block(qp, ql, qr):
        s = (jnp.einsum('bqhc,bsc->bhqs', ql, c_all)
             + jnp.einsum('bqhr,bsr->bhqs', qr, kr_all)).astype(jnp.float32) * scale
        s = jnp.where(chunk_mask(qp, k_pos)[None, None], s, -jnp.inf)
        p = jax.nn.softmax(s, axis=-1).astype(c_all.dtype)
        return jnp.einsum('bhqs,bsc->bqhc', p, c_all)

    o_lat = map_query_blocks(block, pos, q_lat, q_rope)
    o = jnp.einsum('bthc,chv->bthv', o_lat, w_uv).reshape(B, T, MLA_HEADS * MLA_V)
    return o @ w_o, ckv, kr


def dsa_mixer(h, pos, past_k, past_v, past_ki, w_in, g_ki, w_o, rel_bias):
    B, T, _ = h.shape
    q, k, v, qi, ki, wi = jnp.split(h @ w_in, DSA_SPLITS, axis=-1)
    q = q.reshape(B, T, DSA_KV_HEADS, DSA_GROUP, DSA_HEAD_DIM)
    k = k.reshape(B, T, DSA_KV_HEADS, DSA_HEAD_DIM)
    v = v.reshape(B, T, DSA_KV_HEADS, DSA_HEAD_DIM)
    qi = qi.reshape(B, T, IDX_HEADS, IDX_DIM)
    ki = rms_norm(ki, g_ki)
    if past_k is None:
        k_all, v_all, ki_all = k, v, ki
    else:
        k_all = jnp.concatenate([past_k, k], axis=1)
        v_all = jnp.concatenate([past_v, v], axis=1)
        ki_all = jnp.concatenate([past_ki, ki], axis=1)
    S = k_all.shape[1]
    k_pos = jnp.arange(S, dtype=jnp.int32)
    top_k = min(TOPK_MAX, S // 4)

    def block(qp, qb, qib, wib):
        isc = jnp.einsum('bqhi,bsi->bqhs', qib, ki_all).astype(jnp.float32) * (IDX_DIM ** -0.5)
        isc = jnp.einsum('bqhs,bqh->bqs', jax.nn.relu(isc), wib.astype(jnp.float32) * (IDX_HEADS ** -0.5))
        isc = jnp.where(chunk_mask(qp, k_pos)[None], isc, -jnp.inf)
        _, idx = lax.top_k(isc, top_k)
        k_sel = jax.vmap(lambda kb, ib: kb[ib])(k_all, idx)
        v_sel = jax.vmap(lambda vb, ib: vb[ib])(v_all, idx)
        valid = (idx // CHUNK) <= (qp // CHUNK)[None, :, None]
        bias = rel_bias[t5_bucket(qp[None, :, None] - idx)]
        bias = jnp.moveaxis(bias.reshape(*idx.shape, DSA_KV_HEADS, DSA_GROUP), 2, -1)
        s = (jnp.einsum('bqgrd,bqkgd->bqgrk', qb, k_sel).astype(jnp.float32) * (DSA_HEAD_DIM ** -0.5)
             + bias.astype(jnp.float32))
        s = jnp.where(valid[:, :, None, None, :], s, -jnp.inf)
        p = jax.nn.softmax(s, axis=-1).astype(v_sel.dtype)
        return jnp.einsum('bqgrk,bqkgd->bqgrd', p, v_sel)

    o = map_query_blocks(block, pos, q, qi, wi).reshape(B, T, DSA_Q_COLS)
    return o @ w_o, k, v, ki


def setup_inputs(seed: int = 0) -> dict:
    key = jax.random.key(seed)
    k = jax.random.split(key, 22)

    def nrm(kk, shape, scale=1.0):
        return jax.random.normal(kk, shape, jnp.float32) * scale

    def gain(kk, shape):
        return 1.0 + nrm(kk, shape, 0.02)

    return {
        'x_prompt': nrm(k[0], (BATCH, SEQ, D_MODEL)),
        'x_sample': nrm(k[1], (DEC_BATCH, DEC_SEQ, D_MODEL)),
        'cache_mla_latent': nrm(k[2], (N_MLA, DEC_BATCH, PAST_LEN, MLA_KV_LORA)),
        'cache_mla_krope': nrm(k[3], (N_MLA, DEC_BATCH, PAST_LEN, MLA_ROPE)),
        'cache_dsa_k': nrm(k[4], (N_DSA, DEC_BATCH, PAST_LEN, DSA_KV_HEADS, DSA_HEAD_DIM)),
        'cache_dsa_v': nrm(k[5], (N_DSA, DEC_BATCH, PAST_LEN, DSA_KV_HEADS, DSA_HEAD_DIM)),
        'cache_dsa_kidx': nrm(k[6], (N_DSA, DEC_BATCH, PAST_LEN, IDX_DIM)),
        'norm_g': gain(k[7], (DEPTH, 3, D_MODEL)),
        'norm_f': gain(k[8], (D_MODEL,)),
        'ffn_w_in': nrm(k[9], (DEPTH, 2, D_MODEL, 2 * D_FF), D_MODEL ** -0.5),
        'ffn_w_out': nrm(k[10], (DEPTH, 2, D_FF, D_MODEL), D_FF ** -0.5),
        'mla_w_down': nrm(k[11], (N_MLA, D_MODEL, MLA_Q_LORA + MLA_KV_LORA + MLA_ROPE), D_MODEL ** -0.5),
        'mla_g_q': gain(k[12], (N_MLA, MLA_Q_LORA)),
        'mla_g_kv': gain(k[13], (N_MLA, MLA_KV_LORA)),
        'mla_w_uq': nrm(k[14], (N_MLA, MLA_Q_LORA, MLA_HEADS, MLA_NOPE + MLA_ROPE), MLA_Q_LORA ** -0.5),
        'mla_w_uk': nrm(k[15], (N_MLA, MLA_KV_LORA, MLA_HEADS, MLA_NOPE), MLA_KV_LORA ** -0.5),
        'mla_w_uv': nrm(k[16], (N_MLA, MLA_KV_LORA, MLA_HEADS, MLA_V), MLA_KV_LORA ** -0.5),
        'mla_w_o': nrm(k[17], (N_MLA, MLA_HEADS * MLA_V, D_MODEL), (MLA_HEADS * MLA_V) ** -0.5),
        'dsa_w_in': nrm(k[18], (N_DSA, D_MODEL, DSA_IN_COLS), D_MODEL ** -0.5),
        'dsa_g_kidx': gain(k[19], (N_DSA, IDX_DIM)),
        'dsa_w_o': nrm(k[20], (N_DSA, DSA_Q_COLS, D_MODEL), DSA_Q_COLS ** -0.5),
        'rel_bias': nrm(k[21], (REL_BUCKETS, DSA_HEADS), 0.2),
    }


def reference(x_prompt, x_sample, cache_mla_latent, cache_mla_krope, cache_dsa_k, cache_dsa_v,
              cache_dsa_kidx, norm_g, norm_f, ffn_w_in, ffn_w_out, mla_w_down, mla_g_q, mla_g_kv,
              mla_w_uq, mla_w_uk, mla_w_uv, mla_w_o, dsa_w_in, dsa_g_kidx, dsa_w_o, rel_bias):
    def trunk(x, pos, use_cache):
        new_c, new_kr, new_k, new_v, new_ki = [], [], [], [], []
        for i in range(DEPTH):
            j = i // N_MIXERS
            x = x + 0.5 * swiglu(rms_norm(x, norm_g[i, 0]), ffn_w_in[i, 0], ffn_w_out[i, 0])
            h = rms_norm(x, norm_g[i, 1])
            if i % N_MIXERS == 0:
                past_c = cache_mla_latent[j] if use_cache else None
                past_kr = cache_mla_krope[j] if use_cache else None
                o, c, kr = mla_mixer(h, pos, past_c, past_kr, mla_w_down[j], mla_g_q[j], mla_g_kv[j],
                                     mla_w_uq[j], mla_w_uk[j], mla_w_uv[j], mla_w_o[j])
                new_c.append(c)
                new_kr.append(kr)
            else:
                past_k = cache_dsa_k[j] if use_cache else None
                past_v = cache_dsa_v[j] if use_cache else None
                past_ki = cache_dsa_kidx[j] if use_cache else None
                o, kk, vv, ki = dsa_mixer(h, pos, past_k, past_v, past_ki, dsa_w_in[j], dsa_g_kidx[j],
                                          dsa_w_o[j], rel_bias)
                new_k.append(kk)
                new_v.append(vv)
                new_ki.append(ki)
            x = x + o
            x = x + 0.5 * swiglu(rms_norm(x, norm_g[i, 2]), ffn_w_in[i, 1], ffn_w_out[i, 1])
        return (rms_norm(x, norm_f), jnp.stack(new_c), jnp.stack(new_kr),
                jnp.stack(new_k), jnp.stack(new_v), jnp.stack(new_ki))

    prompt_pos = jnp.arange(x_prompt.shape[1], dtype=jnp.int32)
    sample_pos = cache_mla_latent.shape[2] + jnp.arange(x_sample.shape[1], dtype=jnp.int32)
    y_prompt, p_c, p_kr, p_k, p_v, p_ki = trunk(x_prompt, prompt_pos, False)
    y_sample, s_c, s_kr, s_k, s_v, s_ki = trunk(x_sample, sample_pos, True)
    return (y_prompt, y_sample, p_c, p_kr, p_k, p_v, p_ki, s_c, s_kr, s_k, s_v, s_ki)
```

```python
import functools
import math

import jax
import jax.numpy as jnp
from jax import lax
from jax.experimental import pallas as pl
from jax.experimental.pallas import tpu as pltpu

BF = jnp.bfloat16
F32 = jnp.float32

CHUNK = 64
ROPE_THETA = 10000.0
NORM_EPS = 1e-6
TOPK_MAX = 256
REL_BUCKETS = 32
REL_MAX_DIST = 128

LANES = 128
KEY_TILE = 2 * LANES
VMEM_LIMIT = 56 * 1024 * 1024
NEG = -1e30
INT_MIN = -(2 ** 31)
CHUNK_SHIFT = CHUNK.bit_length() - 1
assert 1 << CHUNK_SHIFT == CHUNK

NT = (((1,), (1,)), ((), ()))


def _cparams(*sem):
    return pltpu.CompilerParams(dimension_semantics=sem, vmem_limit_bytes=VMEM_LIMIT)


def _rms(x, g):
    ms = jnp.mean(x * x, axis=-1, keepdims=True)
    return x * lax.rsqrt(ms + NORM_EPS) * g


def _dot(a, b):
    return jnp.dot(a, b, preferred_element_type=F32)


def _dot_nt(a, b):
    return lax.dot_general(a, b, NT, preferred_element_type=F32)


def _wspec(prefix, k, tn, col0=0):
    return pl.BlockSpec((None,) * len(prefix) + (k, tn),
                        lambda i, j: tuple(prefix) + (0, col0 + j))


def _ffn_in_kernel(x_ref, g_ref, wg_ref, wu_ref, o_ref, h_ref):
    @pl.when(pl.program_id(1) == 0)
    def _():
        h_ref[...] = _rms(x_ref[...], g_ref[...]).astype(BF)

    h = h_ref[...]
    gate = _dot(h, wg_ref[...].astype(BF))
    up = _dot(h, wu_ref[...].astype(BF))
    o_ref[...] = (gate / (1.0 + jnp.exp(-gate)) * up).astype(BF)


def _ffn_in(x, g3, gi, w_in, prefix, tm, tf):
    m, d = x.shape
    d_ff = w_in.shape[-1] // 2
    nf = d_ff // tf
    return pl.pallas_call(
        _ffn_in_kernel,
        out_shape=jax.ShapeDtypeStruct((m, d_ff), BF),
        grid=(m // tm, nf),
        in_specs=[
            pl.BlockSpec((tm, d), lambda i, j: (i, 0)),
            pl.BlockSpec((None, 1, d), lambda i, j: (gi, 0, 0)),
            _wspec(prefix, d, tf),
            _wspec(prefix, d, tf, nf),
        ],
        out_specs=pl.BlockSpec((tm, tf), lambda i, j: (i, j)),
        scratch_shapes=[pltpu.VMEM((tm, d), BF)],
        compiler_params=_cparams("parallel", "arbitrary"),
        name="ffn_in",
    )(x, g3, w_in, w_in)


def _mm_res_kernel(a_ref, w_ref, x_ref, o_ref, *, scale):
    acc = _dot(a_ref[...], w_ref[...].astype(BF))
    o_ref[...] = x_ref[...] + scale * acc


def _mm_res(a, w, prefix, x, scale, tm, tn):
    m, k = a.shape
    n = x.shape[1]
    return pl.pallas_call(
        functools.partial(_mm_res_kernel, scale=scale),
        out_shape=jax.ShapeDtypeStruct((m, n), F32),
        grid=(m // tm, n // tn),
        in_specs=[
            pl.BlockSpec((tm, k), lambda i, j: (i, 0)),
            _wspec(prefix, k, tn),
            pl.BlockSpec((tm, tn), lambda i, j: (i, j)),
        ],
        out_specs=pl.BlockSpec((tm, tn), lambda i, j: (i, j)),
        compiler_params=_cparams("parallel", "arbitrary"),
        name="mm_res",
    )(a, w, x)


def _norm_mm_kernel(x_ref, g_ref, w_ref, o_ref, h_ref, *, scale_blocks):
    j = pl.program_id(1)

    @pl.when(j == 0)
    def _():
        h_ref[...] = _rms(x_ref[...], g_ref[...]).astype(BF)

    acc = _dot(h_ref[...], w_ref[...].astype(BF))
    if scale_blocks is not None:
        nb, val = scale_blocks
        acc = acc * jnp.where(j < nb, val, 1.0).astype(F32)
    o_ref[...] = acc.astype(o_ref.dtype)


def _norm_mm(x, g3, gi, w, prefix, col0, n_blk, tn, out_dtype, tm, scale_blocks=None):
    m, d = x.shape
    return pl.pallas_call(
        functools.partial(_norm_mm_kernel, scale_blocks=scale_blocks),
        out_shape=jax.ShapeDtypeStruct((m, n_blk * tn), out_dtype),
        grid=(m // tm, n_blk),
        in_specs=[
            pl.BlockSpec((tm, d), lambda i, j: (i, 0)),
            pl.BlockSpec((None, 1, d), lambda i, j: (gi, 0, 0)),
            _wspec(prefix, d, tn, col0),
        ],
        out_specs=pl.BlockSpec((tm, tn), lambda i, j: (i, j)),
        scratch_shapes=[pltpu.VMEM((tm, d), BF)],
        compiler_params=_cparams("parallel", "arbitrary"),
        name="norm_mm",
    )(x, g3, w)


def _mm_kernel(a_ref, w_ref, o_ref):
    o_ref[...] = _dot(a_ref[...], w_ref[...].astype(BF)).astype(o_ref.dtype)


def _mm(a, w, m_rows, k_cols, tm, tn, out_dtype):
    n = w.shape[1]
    return pl.pallas_call(
        _mm_kernel,
        out_shape=jax.ShapeDtypeStruct((m_rows, n), out_dtype),
        grid=(m_rows // tm, n // tn),
        in_specs=[
            pl.BlockSpec((tm, k_cols), lambda i, j: (i, 0)),
            pl.BlockSpec((k_cols, tn), lambda i, j: (0, j)),
        ],
        out_specs=pl.BlockSpec((tm, tn), lambda i, j: (i, j)),
        compiler_params=_cparams("parallel", "arbitrary"),
        name="mm",
    )(a, w)


def _final_norm_kernel(x_ref, g_ref, o_ref):
    o_ref[...] = _rms(x_ref[...], g_ref[...])


def _final_norm(x, g, tm):
    m, d = x.shape
    return pl.pallas_call(
        _final_norm_kernel,
        out_shape=jax.ShapeDtypeStruct((m, d), F32),
        grid=(m // tm,),
        in_specs=[pl.BlockSpec((tm, d), lambda i: (i, 0)),
                  pl.BlockSpec((1, d), lambda i: (0, 0))],
        out_specs=pl.BlockSpec((tm, d), lambda i: (i, 0)),
        compiler_params=_cparams("parallel"),
        name="final_norm",
    )(x, g.reshape(1, d))


def _mla_post_kernel(d_ref, kr2_ref, gq_ref, gkv_ref, c_ref, s_ref,
                     cq_ref, ckv_ref, aug_ref, kr_ref, *, q_lora, kv_lora, rope):
    d = d_ref[...]
    cq_ref[...] = _rms(d[:, :q_lora], gq_ref[...]).astype(BF)
    ckv = _rms(d[:, q_lora:q_lora + kv_lora], gkv_ref[...])
    ckv_ref[...] = ckv
    k2 = kr2_ref[...]
    kr = k2 * c_ref[...] + pltpu.roll(k2, rope, 1) * s_ref[...]
    aug_ref[:, :kv_lora] = ckv.astype(BF)
    aug_ref[:, kv_lora:] = kr.astype(BF)
    kr_ref[...] = kr[:, :rope]


def _mla_post(d, kr2, g_q, g_kv, cos_t, sin_t, tm, q_lora, kv_lora, rope):
    m = d.shape[0]
    row = lambda w: pl.BlockSpec((tm, w), lambda i: (i, 0))
    vec = lambda w: pl.BlockSpec((1, w), lambda i: (0, 0))
    return pl.pallas_call(
        functools.partial(_mla_post_kernel, q_lora=q_lora, kv_lora=kv_lora, rope=rope),
        out_shape=(jax.ShapeDtypeStruct((m, q_lora), BF),
                   jax.ShapeDtypeStruct((m, kv_lora), F32),
                   jax.ShapeDtypeStruct((m, kv_lora + LANES), BF),
                   jax.ShapeDtypeStruct((m, rope), F32)),
        grid=(m // tm,),
        in_specs=[row(d.shape[1]), row(LANES), vec(q_lora), vec(kv_lora), row(LANES), row(LANES)],
        out_specs=(row(q_lora), row(kv_lora), row(kv_lora + LANES), row(rope)),
        compiler_params=_cparams("parallel"),
        name="mla_post",
    )(d, kr2, g_q.reshape(1, -1), g_kv.reshape(1, -1), cos_t, sin_t)


def _mla_qup_kernel(cq_ref, w_ref, c_ref, s_ref, o_ref, *, rope, scale):
    acc = _dot(cq_ref[...], w_ref[...])
    lo = acc[:, :LANES]
    up = acc[:, LANES:]
    rot = up * c_ref[...] + pltpu.roll(up, rope, 1) * s_ref[...]
    o_ref[:, :LANES] = (lo * scale).astype(BF)
    o_ref[:, LANES:] = (rot * scale).astype(BF)


def _mla_qup(cq, wq, cos_t, sin_t, tm, rope, scale):
    m, k = cq.shape
    heads = wq.shape[0]
    return pl.pallas_call(
        functools.partial(_mla_qup_kernel, rope=rope, scale=scale),
        out_shape=jax.ShapeDtypeStruct((m, heads * 2 * LANES), BF),
        grid=(m // tm, heads),
        in_specs=[
            pl.BlockSpec((tm, k), lambda i, h: (i, 0)),
            pl.BlockSpec((None, k, 2 * LANES), lambda i, h: (h, 0, 0)),
            pl.BlockSpec((tm, LANES), lambda i, h: (i, 0)),
            pl.BlockSpec((tm, LANES), lambda i, h: (i, 0)),
        ],
        out_specs=pl.BlockSpec((tm, 2 * LANES), lambda i, h: (i, h)),
        compiler_params=_cparams("parallel", "arbitrary"),
        name="mla_qup",
    )(cq, wq, cos_t, sin_t)


def _mla_attn_prompt_kernel(q_ref, k_ref, v_ref, o_ref, *, tq, tk):
    i = pl.program_id(2)
    q = q_ref[...]
    q0 = i * tq
    n_tiles = (q0 + tq + tk - 1) // tk
    qchunk = (q0 + lax.broadcasted_iota(jnp.int32, (tq, 1), 0)) >> CHUNK_SHIFT
    dv = v_ref.shape[1]

    def body(t, carry):
        m, l, acc = carry
        ks = pl.multiple_of(t * tk, tk)
        s = _dot_nt(q, k_ref[pl.ds(ks, tk), :])
        kchunk = (ks + lax.broadcasted_iota(jnp.int32, (1, tk), 1)) >> CHUNK_SHIFT
        s = jnp.where(kchunk <= qchunk, s, NEG)
        m_new = jnp.maximum(m, jnp.max(s, axis=-1, keepdims=True))
        alpha = jnp.exp(m - m_new)
        p = jnp.exp(s - m_new)
        l = alpha * l + jnp.sum(p, axis=-1, keepdims=True)
        acc = alpha * acc + _dot(p.astype(BF), v_ref[pl.ds(ks, tk), :])
        return m_new, l, acc

    init = (jnp.full((tq, 1), NEG, F32), jnp.zeros((tq, 1), F32), jnp.zeros((tq, dv), F32))
    _, l, acc = lax.fori_loop(0, n_tiles, body, init)
    o_ref[...] = (acc / l).astype(BF)


def _mla_attn_prompt(q_arr, k_arr, v_arr, batch, seq, heads, tq, tk):
    assert tq % CHUNK == 0 and seq % tk == 0 and seq % tq == 0
    nt = seq // tq
    dv = v_arr.shape[1] // heads
    return pl.pallas_call(
        functools.partial(_mla_attn_prompt_kernel, tq=tq, tk=tk),
        out_shape=jax.ShapeDtypeStruct((batch * seq, heads * dv), BF),
        grid=(batch, heads, nt),
        in_specs=[
            pl.BlockSpec((tq, 2 * LANES), lambda b, h, i: (b * nt + i, h)),
            pl.BlockSpec((seq, 2 * LANES), lambda b, h, i: (b, h)),
            pl.BlockSpec((seq, dv), lambda b, h, i: (b, h)),
        ],
        out_specs=pl.BlockSpec((tq, dv), lambda b, h, i: (b * nt + i, h)),
        compiler_params=_cparams("parallel", "parallel", "arbitrary"),
        name="mla_attn_prompt",
    )(q_arr, k_arr, v_arr)


def _mla_qlat_kernel(qn_ref, qr_ref, wuk_ref, ql_ref, qro_ref):
    ql_ref[...] = _dot_nt(qn_ref[...], wuk_ref[...]).astype(BF)
    qro_ref[...] = qr_ref[...]


def _mla_qlat(q_arr, wuk_t, row0, rows):
    heads, kv_lora, _ = wuk_t.shape
    rb = row0 // rows
    return pl.pallas_call(
        _mla_qlat_kernel,
        out_shape=(jax.ShapeDtypeStruct((heads, rows, kv_lora), BF),
                   jax.ShapeDtypeStruct((heads, rows, LANES), BF)),
        grid=(heads,),
        in_specs=[
            pl.BlockSpec((rows, LANES), lambda h: (rb, 2 * h)),
            pl.BlockSpec((rows, LANES), lambda h: (rb, 2 * h + 1)),
            pl.BlockSpec((None, kv_lora, LANES), lambda h: (h, 0, 0)),
        ],
        out_specs=(pl.BlockSpec((None, rows, kv_lora), lambda h: (h, 0, 0)),
                   pl.BlockSpec((None, rows, LANES), lambda h: (h, 0, 0))),
        compiler_params=_cparams("parallel"),
        name="mla_qlat",
    )(q_arr, q_arr, wuk_t)


def _mla_attn_sample_kernel(ql_ref, qr_ref, c_ref, kr_ref, aug_ref, o_ref, *, heads, t, kv_lora, per_blk):
    b = pl.program_id(0)
    rows = heads * t
    ql = ql_ref[...].reshape(rows, kv_lora)
    qr = qr_ref[...].reshape(rows, LANES)
    c = c_ref[...].astype(BF)
    s_c = _dot_nt(ql, c) + _dot_nt(qr, kr_ref[...])
    cn = aug_ref[:, :kv_lora]
    s_n = _dot_nt(ql, cn) + _dot_nt(qr, aug_ref[:, kv_lora:])
    owner = lax.broadcasted_iota(jnp.int32, (1, LANES), 1) // t
    s_n = jnp.where(owner == b % per_blk, s_n, NEG)
    m = jnp.maximum(jnp.max(s_c, axis=-1, keepdims=True), jnp.max(s_n, axis=-1, keepdims=True))
    p_c = jnp.exp(s_c - m)
    p_n = jnp.exp(s_n - m)
    l = jnp.sum(p_c, axis=-1, keepdims=True) + jnp.sum(p_n, axis=-1, keepdims=True)
    o = (_dot(p_c.astype(BF), c) + _dot(p_n.astype(BF), cn)) / l
    o_ref[...] = o.reshape(heads, t, kv_lora).astype(BF)


def _mla_attn_sample(q_lat, q_rope, cache_c, cache_kr_pad, aug, layer, row0, streams, t):
    heads, _, kv_lora = q_lat.shape
    past = cache_c.shape[2]
    assert past % CHUNK == 0 and t <= CHUNK and LANES % t == 0 and row0 % LANES == 0
    per_blk = LANES // t
    return pl.pallas_call(
        functools.partial(_mla_attn_sample_kernel, heads=heads, t=t, kv_lora=kv_lora, per_blk=per_blk),
        out_shape=jax.ShapeDtypeStruct((heads, streams * t, kv_lora), BF),
        grid=(streams,),
        in_specs=[
            pl.BlockSpec((heads, t, kv_lora), lambda b: (0, b, 0)),
            pl.BlockSpec((heads, t, LANES), lambda b: (0, b, 0)),
            pl.BlockSpec((None, None, past, kv_lora), lambda b: (layer, b, 0, 0)),
            pl.BlockSpec((None, None, past, LANES), lambda b: (layer, b, 0, 0)),
            pl.BlockSpec((LANES, kv_lora + LANES), lambda b: (row0 // LANES + b // per_blk, 0)),
        ],
        out_specs=pl.BlockSpec((heads, t, kv_lora), lambda b: (0, b, 0)),
        compiler_params=_cparams("parallel"),
        name="mla_attn_sample",
    )(q_lat, q_rope, cache_c, cache_kr_pad, aug)


def _mla_oup_kernel(ol_ref, w_ref, o_ref):
    o_ref[...] = _dot(ol_ref[...], w_ref[...]).astype(BF)


def _mla_oup(o_lat, wuv_t):
    heads, rows, kv_lora = o_lat.shape
    dv = wuv_t.shape[2]
    return pl.pallas_call(
        _mla_oup_kernel,
        out_shape=jax.ShapeDtypeStruct((rows, heads * dv), BF),
        grid=(heads,),
        in_specs=[pl.BlockSpec((None, rows, kv_lora), lambda h: (h, 0, 0)),
                  pl.BlockSpec((None, kv_lora, dv), lambda h: (h, 0, 0))],
        out_specs=pl.BlockSpec((rows, dv), lambda h: (0, h)),
        compiler_params=_cparams("parallel"),
        name="mla_oup",
    )(o_lat, wuv_t)


def _dsa_small_kernel(x_ref, g_ref, w_ref, gk_ref, o_ref, *, idx_dim, wi_scale):
    h = _rms(x_ref[...], g_ref[...]).astype(BF)
    y = _dot(h, w_ref[...])
    lo = y[:, :LANES]
    lane = lax.broadcasted_iota(jnp.int32, lo.shape, 1)
    ms = jnp.sum(jnp.where(lane < idx_dim, lo * lo, 0.0), axis=-1, keepdims=True) / idx_dim
    o_ref[:, :LANES] = lo * lax.rsqrt(ms + NORM_EPS) * gk_ref[...]
    o_ref[:, LANES:] = y[:, LANES:] * wi_scale


def _dsa_small(x, g3, gi, w_small, gk2, tm, idx_dim, wi_scale):
    m, d = x.shape
    return pl.pallas_call(
        functools.partial(_dsa_small_kernel, idx_dim=idx_dim, wi_scale=wi_scale),
        out_shape=jax.ShapeDtypeStruct((m, 2 * LANES), F32),
        grid=(m // tm,),
        in_specs=[pl.BlockSpec((tm, d), lambda i: (i, 0)),
                  pl.BlockSpec((None, 1, d), lambda i: (gi, 0, 0)),
                  pl.BlockSpec((d, 2 * LANES), lambda i: (0, 0)),
                  pl.BlockSpec((1, LANES), lambda i: (0, 0))],
        out_specs=pl.BlockSpec((tm, 2 * LANES), lambda i: (i, 0)),
        compiler_params=_cparams("parallel"),
        name="dsa_small",
    )(x, g3, w_small, gk2)


def _bias_tab_kernel(rb_ref, o_ref):
    h = pl.program_id(0)
    half = REL_BUCKETS // 2
    max_exact = half // 2
    rows = o_ref.shape[1]
    i = lax.broadcasted_iota(jnp.int32, (rows, LANES), 0)
    jj = lax.broadcasted_iota(jnp.int32, (rows, LANES), 1)
    far = rb_ref[half - 1, h]
    for v in range(o_ref.shape[0]):
        for side_idx, off in enumerate((v, v - 1)):
            dist = off * LANES + i - jj
            side = jnp.where(dist < 0, half, 0)
            a = jnp.abs(dist)
            a_f = jnp.maximum(a, 1).astype(F32)
            large = max_exact + (jnp.log(a_f / max_exact) / math.log(REL_MAX_DIST / max_exact)
                                 * (half - max_exact)).astype(jnp.int32)
            large = jnp.minimum(large, half - 1)
            bucket = side + jnp.where(a < max_exact, a, large)
            bias = jnp.zeros((rows, LANES), F32)
            for bk in range(REL_BUCKETS):
                bias = jnp.where(bucket == bk, rb_ref[bk, h], bias)
            o_ref[v, :, side_idx * LANES:(side_idx + 1) * LANES] = bias - far


def _bias_tab(rel_bias, rows):
    heads = rel_bias.shape[1]
    return pl.pallas_call(
        _bias_tab_kernel,
        out_shape=jax.ShapeDtypeStruct((3, heads, rows, KEY_TILE), F32),
        grid=(heads,),
        in_specs=[pl.BlockSpec(memory_space=pltpu.SMEM)],
        out_specs=pl.BlockSpec((3, None, rows, KEY_TILE), lambda h: (0, h, 0, 0)),
        compiler_params=_cparams("parallel"),
        name="bias_tab",
    )(rel_bias)


def _dsa_attn_kernel(q_ref, qi_ref, wi_ref, k_ref, v_ref, kd_ref, tab_ref, o_ref,
                     key_ref, mb_ref, qim_ref, *, tq, q_pos0, s_real, topk, kv_heads, group, idx_heads, idx_dim):
    i = pl.program_id(1)
    q0 = q_pos0 + i * tq
    qb = q0 // LANES
    n_valid = qb // 2 + 1
    n_far = jnp.maximum((qb + 1) // 2 - 1, 0)
    qchunk = (q0 + lax.broadcasted_iota(jnp.int32, (tq, 1), 0)) >> CHUNK_SHIFT
    dh = LANES

    lane = lax.broadcasted_iota(jnp.int32, (tq, LANES), 1)
    per_pair = LANES // idx_dim
    for h in range(idx_heads):
        pair = qi_ref[:, (h // per_pair) * LANES:(h // per_pair + 1) * LANES].astype(F32)
        sub = h % per_pair
        keep = (lane >= sub * idx_dim) & (lane < (sub + 1) * idx_dim)
        qim_ref[h] = jnp.where(keep, pair, 0.0).astype(BF)
    wi = wi_ref[...]

    def tile_start(t):
        return pl.multiple_of(t * KEY_TILE, KEY_TILE)

    def idx_body(t, carry):
        ks = tile_start(t)
        kd = kd_ref[pl.ds(ks, KEY_TILE), :].astype(BF)
        acc = jnp.zeros((tq, KEY_TILE), F32)
        for h in range(idx_heads):
            sc = _dot_nt(qim_ref[h], kd)
            acc = acc + jnp.maximum(sc, 0.0) * wi[:, h:h + 1]
        kpos = ks + lax.broadcasted_iota(jnp.int32, (tq, KEY_TILE), 1)
        valid = ((kpos >> CHUNK_SHIFT) <= qchunk) & (kpos < s_real)
        bits = lax.bitcast_convert_type(acc, jnp.int32)
        key = bits ^ ((bits >> 31) & 0x7FFFFFFF)
        key_ref[:, pl.ds(ks, KEY_TILE)] = jnp.where(valid, key, INT_MIN)
        return carry

    lax.fori_loop(0, n_valid, idx_body, 0)

    def count_ge(cand):
        def body(t, acc):
            c = jnp.where(key_ref[:, pl.ds(tile_start(t), KEY_TILE)] >= cand, 1.0, 0.0)
            return acc + c[:, :LANES] + c[:, LANES:]
        acc = lax.fori_loop(0, n_valid, body, jnp.zeros((tq, LANES), F32))
        return jnp.sum(acc, axis=-1, keepdims=True)

    thr = jnp.where(count_ge(jnp.zeros((tq, 1), jnp.int32)) >= topk, 0, INT_MIN).astype(jnp.int32)

    def bit_body(b, thr):
        cand = thr | (jnp.int32(1) << (30 - b))
        return jnp.where(count_ge(cand) >= topk, cand, thr)

    thr = lax.fori_loop(0, 31, bit_body, thr)
    thr = jnp.maximum(thr, INT_MIN + 1)

    def mb_body(t, carry):
        ks = tile_start(t)
        mb_ref[:, pl.ds(ks, KEY_TILE)] = jnp.where(key_ref[:, pl.ds(ks, KEY_TILE)] >= thr, 0.0, NEG)
        return carry

    lax.fori_loop(0, n_valid, mb_body, 0)

    for g in range(kv_heads):
        qg = jnp.concatenate(
            [q_ref[:, (group * g + r) * dh:(group * g + r + 1) * dh] for r in range(group)], axis=0)

        def tile(t, carry, near):
            m, l, acc = carry
            ks = tile_start(t)
            s = _dot_nt(qg, k_ref[pl.ds(ks, KEY_TILE), g * dh:(g + 1) * dh]).reshape(group, tq, KEY_TILE)
            s = s + mb_ref[:, pl.ds(ks, KEY_TILE)][None]
            if near:
                s = s + tab_ref[qb - 2 * t, group * g:group * (g + 1)]
            m_new = jnp.maximum(m, jnp.max(s, axis=-1, keepdims=True))
            alpha = jnp.exp(m - m_new)
            p = jnp.exp(s - m_new)
            l = alpha * l + jnp.sum(p, axis=-1, keepdims=True)
            pv = _dot(p.reshape(group * tq, KEY_TILE).astype(BF), v_ref[pl.ds(ks, KEY_TILE), g * dh:(g + 1) * dh])
            acc = alpha * acc + pv.reshape(group, tq, dh)
            return m_new, l, acc

        init = (jnp.full((group, tq, 1), NEG, F32), jnp.zeros((group, tq, 1), F32),
                jnp.zeros((group, tq, dh), F32))
        carry = lax.fori_loop(0, n_far, functools.partial(tile, near=False), init)
        _, l, acc = lax.fori_loop(n_far, n_valid, functools.partial(tile, near=True), carry)
        out = acc / l
        for r in range(group):
            o_ref[:, (group * g + r) * dh:(group * g + r + 1) * dh] = out[r].astype(BF)


def _dsa_attn(qkv, small, k_arr, v_arr, kd_arr, kv_col, tab, *, batch, t, tq, row0, s_pad, s_real, q_pos0,
              heads, kv_heads, idx_heads, idx_dim):
    nt = t // tq
    assert q_pos0 % LANES == 0 and (tq % LANES == 0 or nt == 1)
    assert s_pad % KEY_TILE == 0 and s_pad >= ((q_pos0 + t - 1) // KEY_TILE + 1) * KEY_TILE
    dh = LANES
    group = heads // kv_heads
    qcols = heads * dh
    kvcols = kv_heads * dh
    qicols = idx_heads * idx_dim
    rb0 = row0 // tq
    topk = min(TOPK_MAX, s_real // 4)
    kern = functools.partial(_dsa_attn_kernel, tq=tq, q_pos0=q_pos0, s_real=s_real, topk=topk,
                             kv_heads=kv_heads, group=group, idx_heads=idx_heads, idx_dim=idx_dim)
    kc, vc = kv_col
    return pl.pallas_call(
        kern,
        out_shape=jax.ShapeDtypeStruct((batch * t, qcols), BF),
        grid=(batch, nt),
        in_specs=[
            pl.BlockSpec((tq, qcols), lambda b, i: (rb0 + b * nt + i, 0)),
            pl.BlockSpec((tq, qicols), lambda b, i: (rb0 + b * nt + i, (qcols + 2 * kvcols) // qicols)),
            pl.BlockSpec((tq, LANES), lambda b, i: (rb0 + b * nt + i, 1)),
            pl.BlockSpec((s_pad, kvcols), lambda b, i: (b, kc)),
            pl.BlockSpec((s_pad, kvcols), lambda b, i: (b, vc)),
            pl.BlockSpec((s_pad, LANES), lambda b, i: (b, 0)),
            pl.BlockSpec((3, heads, tq, KEY_TILE), lambda b, i: (0, 0, 0, 0)),
        ],
        out_specs=pl.BlockSpec((tq, qcols), lambda b, i: (b * nt + i, 0)),
        scratch_shapes=[pltpu.VMEM((tq, s_pad), jnp.int32),
                        pltpu.VMEM((tq, s_pad), F32),
                        pltpu.VMEM((idx_heads, tq, LANES), BF)],
        compiler_params=_cparams("parallel", "arbitrary"),
        name="dsa_attn",
    )(qkv, qkv, small, k_arr, v_arr, kd_arr, tab)


def _rope_tables(pos, rope):
    half = rope // 2
    inv_freq = ROPE_THETA ** (-jnp.arange(half, dtype=F32) / half)
    ang = pos.astype(F32)[:, None] * inv_freq[None, :]
    cos, sin = jnp.cos(ang), jnp.sin(ang)
    pad = jnp.zeros((pos.shape[0], LANES - rope), F32)
    return (jnp.concatenate([cos, cos, pad], axis=1), jnp.concatenate([sin, sin, pad], axis=1))


def _swap_halves(w):
    half = w.shape[-1] // 2
    return jnp.concatenate([-w[..., half:], w[..., :half]], axis=-1)


@jax.jit
def _step(x_prompt, x_sample, cache_mla_latent, cache_mla_krope, cache_dsa_k, cache_dsa_v, cache_dsa_kidx,
          norm_g, norm_f, ffn_w_in, ffn_w_out, mla_w_down, mla_g_q, mla_g_kv, mla_w_uq, mla_w_uk, mla_w_uv,
          mla_w_o, dsa_w_in, dsa_g_kidx, dsa_w_o, rel_bias):
    batch, seq, d_model = x_prompt.shape
    streams, dec, _ = x_sample.shape
    depth = norm_g.shape[0]
    past = cache_mla_latent.shape[2]
    kv_lora = cache_mla_latent.shape[3]
    rope = cache_mla_krope.shape[3]
    q_lora = mla_w_uq.shape[1]
    mla_heads = mla_w_uq.shape[2]
    nope = mla_w_uk.shape[3]
    kv_heads, dh = cache_dsa_k.shape[3], cache_dsa_k.shape[4]
    idx_dim = cache_dsa_kidx.shape[3]
    dsa_heads = dsa_w_o.shape[1] // dh
    idx_heads = rel_bias.shape[1]
    assert nope == LANES and dh == LANES and 2 * rope == LANES and mla_w_uv.shape[3] == LANES
    assert 2 * idx_dim == LANES and idx_heads <= LANES
    mp, ms = batch * seq, streams * dec
    m = mp + ms
    tm = m // 8
    assert tm * 8 == m and tm % 16 == 0
    qcols, kvcols, qicols = dsa_heads * dh, kv_heads * dh, idx_heads * idx_dim
    big = qcols + 2 * kvcols + qicols

    x = jnp.concatenate([x_prompt.reshape(mp, d_model), x_sample.reshape(ms, d_model)], axis=0)
    g3 = norm_g.reshape(depth * 3, 1, d_model)
    pos = jnp.concatenate([jnp.tile(jnp.arange(seq, dtype=jnp.int32), batch),
                           jnp.tile(past + jnp.arange(dec, dtype=jnp.int32), streams)])
    cos_t, sin_t = _rope_tables(pos, rope)
    tab = _bias_tab(rel_bias, LANES)
    s_pad_s = ((past + dec - 1) // KEY_TILE + 1) * KEY_TILE

    def ffn(x, li, s):
        act = _ffn_in(x, g3, li * 3 + 2 * s, ffn_w_in, (li, s), tm, 256)
        return _mm_res(act, ffn_w_out, (li, s), x, 0.5, tm, 256)

    new_c, new_kr, new_k, new_v, new_ki = [], [], [], [], []
    for li in range(depth):
        j = li // 2
        gi = li * 3 + 1
        x = ffn(x, li, 0)
        if li % 2 == 0:
            w_kr = mla_w_down[j, :, q_lora + kv_lora:]
            w_kr2 = jnp.concatenate([w_kr, _swap_halves(w_kr)], axis=1)[None]
            d = _norm_mm(x, g3, gi, mla_w_down, (j,), 0, (q_lora + kv_lora) // 512, 512, F32, tm)
            kr2 = _norm_mm(x, g3, gi, w_kr2, (0,), 0, 1, LANES, F32, tm)
            cq, ckv, aug, kr = _mla_post(d, kr2, mla_g_q[j], mla_g_kv[j], cos_t, sin_t, tm // 2,
                                         q_lora, kv_lora, rope)
            new_c.append(ckv)
            new_kr.append(kr)
            uq = mla_w_uq[j]
            uq_rope = uq[..., nope:]
            wq = jnp.concatenate([uq[..., :nope], uq_rope, _swap_halves(uq_rope)], axis=-1)
            wq = jnp.transpose(wq, (1, 0, 2)).astype(BF)
            scale = (nope + rope) ** -0.5
            q_arr = _mla_qup(cq, wq, cos_t, sin_t, tm, rope, scale)
            w_kexp = jnp.zeros((kv_lora + LANES, mla_heads, 2 * LANES), F32)
            w_kexp = w_kexp.at[:kv_lora, :, :nope].set(mla_w_uk[j])
            w_kexp = w_kexp.at[kv_lora:kv_lora + rope, :, nope:nope + rope].set(
                jnp.broadcast_to(jnp.eye(rope, dtype=F32)[:, None, :], (rope, mla_heads, rope)))
            w_kexp = w_kexp.reshape(kv_lora + LANES, mla_heads * 2 * LANES).astype(BF)
            k_arr = _mm(aug, w_kexp, mp, kv_lora + LANES, 1024, 512, BF)
            v_arr = _mm(aug, mla_w_uv[j].reshape(kv_lora, mla_heads * LANES).astype(BF), mp, kv_lora, 1024, 512, BF)
            o_p = _mla_attn_prompt(q_arr, k_arr, v_arr, batch, seq, mla_heads, 256, 512)
            wuk_t = jnp.transpose(mla_w_uk[j], (1, 0, 2)).astype(BF)
            wuv_t = jnp.transpose(mla_w_uv[j], (1, 0, 2)).astype(BF)
            q_lat, q_rope = _mla_qlat(q_arr, wuk_t, mp, ms)
            kr_pad = jnp.pad(cache_mla_krope.astype(BF), ((0, 0), (0, 0), (0, 0), (0, LANES - rope)))
            o_lat = _mla_attn_sample(q_lat, q_rope, cache_mla_latent, kr_pad, aug, j, mp, streams, dec)
            o_s = _mla_oup(o_lat, wuv_t)
            o = jnp.concatenate([o_p, o_s], axis=0)
            x = _mm_res(o, mla_w_o, (j,), x, 1.0, tm, 256)
        else:
            qkv = _norm_mm(x, g3, gi, dsa_w_in, (j,), 0, big // 512, 512, BF, tm,
                           scale_blocks=(qcols // 512, dh ** -0.5))
            kv32 = _norm_mm(x, g3, gi, dsa_w_in, (j,), qcols // 512, 2 * kvcols // 512, 512, F32, tm)
            w_ki = dsa_w_in[j, :, big:big + idx_dim]
            w_wi = dsa_w_in[j, :, big + idx_dim:]
            w_small = jnp.concatenate(
                [w_ki, w_ki, w_wi, jnp.zeros((d_model, LANES - idx_heads), F32)], axis=1).astype(BF)
            gk2 = jnp.tile(dsa_g_kidx[j], 2).reshape(1, LANES)
            small = _dsa_small(x, g3, gi, w_small, gk2, tm, idx_dim, (idx_dim ** -0.5) * (idx_heads ** -0.5))
            new_k.append(kv32[:, :kvcols])
            new_v.append(kv32[:, kvcols:])
            new_ki.append(small[:, :idx_dim])
            common = dict(heads=dsa_heads, kv_heads=kv_heads, idx_heads=idx_heads, idx_dim=idx_dim)
            kb = qcols // kvcols
            o_p = _dsa_attn(qkv, small, qkv, qkv, small, (kb, kb + 1), tab, batch=batch, t=seq, tq=LANES,
                            row0=0, s_pad=seq, s_real=seq, q_pos0=0, **common)

            def with_cache(cache, new):
                new = new.reshape(streams, dec, new.shape[-1])
                zpad = jnp.zeros((streams, s_pad_s - past - dec, new.shape[-1]), new.dtype)
                return jnp.concatenate([cache.astype(new.dtype), new, zpad], axis=1).reshape(
                    streams * s_pad_s, new.shape[-1])

            k_all = with_cache(cache_dsa_k[j].reshape(streams, past, kvcols), qkv[mp:, qcols:qcols + kvcols])
            v_all = with_cache(cache_dsa_v[j].reshape(streams, past, kvcols),
                               qkv[mp:, qcols + kvcols:qcols + 2 * kvcols])
            ki_c = cache_dsa_kidx[j]
            kd_all = with_cache(jnp.concatenate([ki_c, ki_c], axis=-1), small[mp:, :LANES])
            o_s = _dsa_attn(qkv, small, k_all, v_all, kd_all, (0, 0), tab, batch=streams, t=dec, tq=dec,
                            row0=mp, s_pad=s_pad_s, s_real=past + dec, q_pos0=past, **common)
            o = jnp.concatenate([o_p, o_s], axis=0)
            x = _mm_res(o, dsa_w_o, (j,), x, 1.0, tm, 256)
        x = ffn(x, li, 1)

    y = _final_norm(x, norm_f, tm // 2)

    def split(rows, *tail):
        a = jnp.stack(rows)
        n = a.shape[0]
        return (a[:, :mp].reshape(n, batch, seq, *tail), a[:, mp:].reshape(n, streams, dec, *tail))

    p_c, s_c = split(new_c, kv_lora)
    p_kr, s_kr = split(new_kr, rope)
    p_k, s_k = split(new_k, kv_heads, dh)
    p_v, s_v = split(new_v, kv_heads, dh)
    p_ki, s_ki = split(new_ki, idx_dim)
    return (y[:mp].reshape(batch, seq, d_model), y[mp:].reshape(streams, dec, d_model),
            p_c, p_kr, p_k, p_v, p_ki, s_c, s_kr, s_k, s_v, s_ki)


def kernel(x_prompt, x_sample, cache_mla_latent, cache_mla_krope, cache_dsa_k, cache_dsa_v, cache_dsa_kidx,
           norm_g, norm_f, ffn_w_in, ffn_w_out, mla_w_down, mla_g_q, mla_g_kv, mla_w_uq, mla_w_uk, mla_w_uv,
           mla_w_o, dsa_w_in, dsa_g_kidx, dsa_w_o, rel_bias):
    return _step(x_prompt, x_sample, cache_mla_latent, cache_mla_krope, cache_dsa_k, cache_dsa_v,
                 cache_dsa_kidx, norm_g, norm_f, ffn_w_in, ffn_w_out, mla_w_down, mla_g_q, mla_g_kv,
                 mla_w_uq, mla_w_uk, mla_w_uv, mla_w_o, dsa_w_in, dsa_g_kidx, dsa_w_o, rel_bias)
```

```python
import functools
import math

import jax
import jax.numpy as jnp
from jax import lax
from jax.experimental import pallas as pl
from jax.experimental.pallas import tpu as pltpu

BF = jnp.bfloat16
F32 = jnp.float32

CHUNK = 64
ROPE_THETA = 10000.0
NORM_EPS = 1e-6
TOPK_MAX = 256
REL_BUCKETS = 32
REL_MAX_DIST = 128

LANES = 128
KEY_TILE = 2 * LANES
VMEM_LIMIT = 56 * 1024 * 1024
NEG = -1e30
INT_MIN = -(2 ** 31)
LOG2E = math.log2(math.e)
CHUNK_SHIFT = CHUNK.bit_length() - 1
assert 1 << CHUNK_SHIFT == CHUNK

NT = (((1,), (1,)), ((), ()))


def _cparams(*sem):
    return pltpu.CompilerParams(dimension_semantics=sem, vmem_limit_bytes=VMEM_LIMIT)


def _rms(x, g):
    ms = jnp.mean(x * x, axis=-1, keepdims=True)
    return x * lax.rsqrt(ms + NORM_EPS) * g


def _dot(a, b):
    return jnp.dot(a, b, preferred_element_type=F32)


def _dot_nt(a, b):
    return lax.dot_general(a, b, NT, preferred_element_type=F32)


def _wspec(prefix, k, tn, col0=0):
    return pl.BlockSpec((None,) * len(prefix) + (k, tn),
                        lambda i, j: tuple(prefix) + (0, col0 + j))


def _ffn_in_kernel(x_ref, g_ref, wg_ref, wu_ref, o_ref, h_ref):
    @pl.when(pl.program_id(1) == 0)
    def _():
        h_ref[...] = _rms(x_ref[...], g_ref[...]).astype(BF)

    h = h_ref[...]
    gate = _dot(h, wg_ref[...].astype(BF))
    up = _dot(h, wu_ref[...].astype(BF))
    o_ref[...] = (gate / (1.0 + jnp.exp(-gate)) * up).astype(BF)


def _ffn_in(x, g3, gi, w_in, prefix, tm, tf):
    m, d = x.shape
    d_ff = w_in.shape[-1] // 2
    nf = d_ff // tf
    return pl.pallas_call(
        _ffn_in_kernel,
        out_shape=jax.ShapeDtypeStruct((m, d_ff), BF),
        grid=(m // tm, nf),
        in_specs=[
            pl.BlockSpec((tm, d), lambda i, j: (i, 0), pipeline_mode=pl.Buffered(1)),
            pl.BlockSpec((None, 1, d), lambda i, j: (gi, 0, 0)),
            _wspec(prefix, d, tf),
            _wspec(prefix, d, tf, nf),
        ],
        out_specs=pl.BlockSpec((tm, tf), lambda i, j: (i, j)),
        scratch_shapes=[pltpu.VMEM((tm, d), BF)],
        compiler_params=_cparams("parallel", "arbitrary"),
        name="ffn_in",
    )(x, g3, w_in, w_in)


def _mm_res_kernel(a_ref, w_ref, x_ref, o_ref, *, scale):
    acc = _dot(a_ref[...], w_ref[...].astype(BF))
    o_ref[...] = x_ref[...] + scale * acc


def _mm_res(a, w, prefix, x, scale, tm, tn):
    m, k = a.shape
    n = x.shape[1]
    return pl.pallas_call(
        functools.partial(_mm_res_kernel, scale=scale),
        out_shape=jax.ShapeDtypeStruct((m, n), F32),
        grid=(m // tm, n // tn),
        in_specs=[
            pl.BlockSpec((tm, k), lambda i, j: (i, 0)),
            _wspec(prefix, k, tn),
            pl.BlockSpec((tm, tn), lambda i, j: (i, j)),
        ],
        out_specs=pl.BlockSpec((tm, tn), lambda i, j: (i, j)),
        compiler_params=_cparams("parallel", "arbitrary"),
        name="mm_res",
    )(a, w, x)


def _norm_mm_kernel(x_ref, g_ref, w_ref, o_ref, h_ref, *, scale_blocks):
    j = pl.program_id(1)

    @pl.when(j == 0)
    def _():
        h_ref[...] = _rms(x_ref[...], g_ref[...]).astype(BF)

    acc = _dot(h_ref[...], w_ref[...].astype(BF))
    if scale_blocks is not None:
        nb, val = scale_blocks
        acc = acc * jnp.where(j < nb, val, 1.0).astype(F32)
    o_ref[...] = acc.astype(o_ref.dtype)


def _norm_mm(x, g3, gi, w, prefix, col0, n_blk, tn, out_dtype, tm, scale_blocks=None):
    m, d = x.shape
    return pl.pallas_call(
        functools.partial(_norm_mm_kernel, scale_blocks=scale_blocks),
        out_shape=jax.ShapeDtypeStruct((m, n_blk * tn), out_dtype),
        grid=(m // tm, n_blk),
        in_specs=[
            pl.BlockSpec((tm, d), lambda i, j: (i, 0)),
            pl.BlockSpec((None, 1, d), lambda i, j: (gi, 0, 0)),
            _wspec(prefix, d, tn, col0),
        ],
        out_specs=pl.BlockSpec((tm, tn), lambda i, j: (i, j)),
        scratch_shapes=[pltpu.VMEM((tm, d), BF)],
        compiler_params=_cparams("parallel", "arbitrary"),
        name="norm_mm",
    )(x, g3, w)


def _mm_kernel(a_ref, w_ref, o_ref):
    o_ref[...] = _dot(a_ref[...], w_ref[...].astype(BF)).astype(o_ref.dtype)


def _mm(a, w, m_rows, k_cols, tm, tn, out_dtype):
    n = w.shape[1]
    return pl.pallas_call(
        _mm_kernel,
        out_shape=jax.ShapeDtypeStruct((m_rows, n), out_dtype),
        grid=(m_rows // tm, n // tn),
        in_specs=[
            pl.BlockSpec((tm, k_cols), lambda i, j: (i, 0)),
            pl.BlockSpec((k_cols, tn), lambda i, j: (0, j)),
        ],
        out_specs=pl.BlockSpec((tm, tn), lambda i, j: (i, j)),
        compiler_params=_cparams("parallel", "arbitrary"),
        name="mm",
    )(a, w)


def _mm_nt_kernel(w_ref, a_ref, o_ref):
    o_ref[...] = _dot_nt(w_ref[...], a_ref[...]).astype(o_ref.dtype)


def _mm_nt(w_t, a, m_rows, k_cols, tn_rows, tm_cols, out_dtype):
    n = w_t.shape[0]
    return pl.pallas_call(
        _mm_nt_kernel,
        out_shape=jax.ShapeDtypeStruct((n, m_rows), out_dtype),
        grid=(n // tn_rows, m_rows // tm_cols),
        in_specs=[
            pl.BlockSpec((tn_rows, k_cols), lambda i, j: (i, 0)),
            pl.BlockSpec((tm_cols, k_cols), lambda i, j: (j, 0)),
        ],
        out_specs=pl.BlockSpec((tn_rows, tm_cols), lambda i, j: (i, j)),
        compiler_params=_cparams("parallel", "arbitrary"),
        name="mm_nt",
    )(w_t, a)


def _final_norm_kernel(x_ref, g_ref, o_ref):
    o_ref[...] = _rms(x_ref[...], g_ref[...])


def _final_norm(x, g, tm):
    m, d = x.shape
    return pl.pallas_call(
        _final_norm_kernel,
        out_shape=jax.ShapeDtypeStruct((m, d), F32),
        grid=(m // tm,),
        in_specs=[pl.BlockSpec((tm, d), lambda i: (i, 0)),
                  pl.BlockSpec((1, d), lambda i: (0, 0))],
        out_specs=pl.BlockSpec((tm, d), lambda i: (i, 0)),
        compiler_params=_cparams("parallel"),
        name="final_norm",
    )(x, g.reshape(1, d))


def _mla_post_kernel(d_ref, kr2_ref, gq_ref, gkv_ref, c_ref, s_ref,
                     cq_ref, ckv_ref, aug_ref, kr_ref, *, q_lora, kv_lora, rope):
    d = d_ref[...]
    cq_ref[...] = _rms(d[:, :q_lora], gq_ref[...]).astype(BF)
    ckv = _rms(d[:, q_lora:q_lora + kv_lora], gkv_ref[...])
    ckv_ref[...] = ckv
    k2 = kr2_ref[...]
    kr = k2 * c_ref[...] + pltpu.roll(k2, rope, 1) * s_ref[...]
    aug_ref[:, :kv_lora] = ckv.astype(BF)
    aug_ref[:, kv_lora:] = kr.astype(BF)
    kr_ref[...] = kr[:, :rope]


def _mla_post(d, kr2, g_q, g_kv, cos_t, sin_t, tm, q_lora, kv_lora, rope):
    m = d.shape[0]
    row = lambda w: pl.BlockSpec((tm, w), lambda i: (i, 0))
    vec = lambda w: pl.BlockSpec((1, w), lambda i: (0, 0))
    return pl.pallas_call(
        functools.partial(_mla_post_kernel, q_lora=q_lora, kv_lora=kv_lora, rope=rope),
        out_shape=(jax.ShapeDtypeStruct((m, q_lora), BF),
                   jax.ShapeDtypeStruct((m, kv_lora), F32),
                   jax.ShapeDtypeStruct((m, kv_lora + LANES), BF),
                   jax.ShapeDtypeStruct((m, rope), F32)),
        grid=(m // tm,),
        in_specs=[row(d.shape[1]), row(LANES), vec(q_lora), vec(kv_lora), row(LANES), row(LANES)],
        out_specs=(row(q_lora), row(kv_lora), row(kv_lora + LANES), row(rope)),
        compiler_params=_cparams("parallel"),
        name="mla_post",
    )(d, kr2, g_q.reshape(1, -1), g_kv.reshape(1, -1), cos_t, sin_t)


def _mla_qup_kernel(cq_ref, w_ref, c_ref, s_ref, o_ref, *, rope, scale):
    acc = _dot(cq_ref[...], w_ref[...])
    lo = acc[:, :LANES]
    up = acc[:, LANES:]
    rot = up * c_ref[...] + pltpu.roll(up, rope, 1) * s_ref[...]
    o_ref[:, :LANES] = (lo * scale).astype(BF)
    o_ref[:, LANES:] = (rot * scale).astype(BF)


def _mla_qup(cq, wq, cos_t, sin_t, tm, rope, scale):
    m, k = cq.shape
    heads = wq.shape[0]
    return pl.pallas_call(
        functools.partial(_mla_qup_kernel, rope=rope, scale=scale),
        out_shape=jax.ShapeDtypeStruct((m, heads * 2 * LANES), BF),
        grid=(m // tm, heads),
        in_specs=[
            pl.BlockSpec((tm, k), lambda i, h: (i, 0)),
            pl.BlockSpec((None, k, 2 * LANES), lambda i, h: (h, 0, 0)),
            pl.BlockSpec((tm, LANES), lambda i, h: (i, 0)),
            pl.BlockSpec((tm, LANES), lambda i, h: (i, 0)),
        ],
        out_specs=pl.BlockSpec((tm, 2 * LANES), lambda i, h: (i, h)),
        compiler_params=_cparams("parallel", "arbitrary"),
        name="mla_qup",
    )(cq, wq, cos_t, sin_t)


def _online_update(s, v_tile, m_ref, l_ref, acc_ref, hs):
    lead = s.shape[:-1]
    m_old = m_ref[hs]
    m_new = jnp.maximum(m_old, jnp.max(s, axis=-1, keepdims=True))
    alpha = jnp.exp2(m_old - m_new)
    p = jnp.exp2(s - m_new)
    l_ref[hs] = alpha * l_ref[hs] + jnp.sum(p, axis=-1, keepdims=True)
    pv = _dot(p.reshape(-1, s.shape[-1]).astype(BF), v_tile)
    acc_ref[hs] = alpha * acc_ref[hs] + pv.reshape(lead + (v_tile.shape[1],))
    m_ref[hs] = m_new


def _online_update_t(s_t, vt_tile, m_ref, l_ref, acc_ref, hs):
    m_old = m_ref[hs]
    m_new = jnp.maximum(m_old, jnp.max(s_t, axis=0, keepdims=True))
    alpha = jnp.exp2(m_old - m_new)
    p = jnp.exp2(s_t - m_new)
    l_ref[hs] = alpha * l_ref[hs] + jnp.sum(p, axis=0, keepdims=True)
    acc_ref[hs] = alpha * acc_ref[hs] + _dot(vt_tile, p.astype(BF))
    m_ref[hs] = m_new


def _mla_attn_prompt_kernel(q_ref, k_ref, vt_ref, o_ref, m_ref, l_ref, acc_ref, *, tq, hp):
    i = pl.program_id(2)
    dk, dv = 2 * LANES, LANES
    m_ref[...] = jnp.full(m_ref.shape, NEG, F32)
    l_ref[...] = jnp.zeros(l_ref.shape, F32)
    acc_ref[...] = jnp.zeros(acc_ref.shape, F32)

    def tile(t, diagonal):
        ks = pl.multiple_of(t * tq, tq)

        def scores(h):
            s_t = _dot_nt(k_ref[pl.ds(ks, tq), h * dk:(h + 1) * dk], q_ref[:, h * dk:(h + 1) * dk])
            if diagonal:
                kc = lax.broadcasted_iota(jnp.int32, (tq, 1), 0) >> CHUNK_SHIFT
                qc = lax.broadcasted_iota(jnp.int32, (1, tq), 1) >> CHUNK_SHIFT
                s_t = jnp.where(kc <= qc, s_t, NEG)
            return s_t

        s_next = scores(0)
        for h in range(hp):
            s_t = s_next
            if h + 1 < hp:
                s_next = scores(h + 1)
            _online_update_t(s_t, vt_ref[h * dv:(h + 1) * dv, pl.ds(ks, tq)], m_ref, l_ref, acc_ref, h)

    def body(t, carry):
        tile(t, False)
        return carry

    lax.fori_loop(0, i, body, 0)
    tile(i, True)
    for h in range(hp):
        o_ref[:, h * dv:(h + 1) * dv] = (acc_ref[h] / l_ref[h]).T.astype(BF)


def _mla_attn_prompt(q_arr, k_arr, vt_arr, batch, seq, heads, tq, hp):
    assert tq % CHUNK == 0 and seq % tq == 0 and heads % hp == 0
    nt = seq // tq
    dk, dv = 2 * LANES, LANES
    return pl.pallas_call(
        functools.partial(_mla_attn_prompt_kernel, tq=tq, hp=hp),
        out_shape=jax.ShapeDtypeStruct((batch * seq, heads * dv), BF),
        grid=(batch, heads // hp, nt),
        in_specs=[
            pl.BlockSpec((tq, hp * dk), lambda b, h, i: (b * nt + i, h)),
            pl.BlockSpec((seq, hp * dk), lambda b, h, i: (b, h)),
            pl.BlockSpec((hp * dv, seq), lambda b, h, i: (h, b)),
        ],
        out_specs=pl.BlockSpec((tq, hp * dv), lambda b, h, i: (b * nt + i, h)),
        scratch_shapes=[pltpu.VMEM((hp, 1, tq), F32), pltpu.VMEM((hp, 1, tq), F32),
                        pltpu.VMEM((hp, dv, tq), F32)],
        compiler_params=_cparams("parallel", "parallel", "arbitrary"),
        name="mla_attn_prompt",
    )(q_arr, k_arr, vt_arr)


def _mla_qlat_kernel(qn_ref, qr_ref, wuk_ref, ql_ref, qro_ref):
    ql_ref[...] = _dot_nt(qn_ref[...], wuk_ref[...]).astype(BF)
    qro_ref[...] = qr_ref[...]


def _mla_qlat(q_arr, wuk_t, row0, rows):
    heads, kv_lora, _ = wuk_t.shape
    rb = row0 // rows
    return pl.pallas_call(
        _mla_qlat_kernel,
        out_shape=(jax.ShapeDtypeStruct((heads, rows, kv_lora), BF),
                   jax.ShapeDtypeStruct((heads, rows, LANES), BF)),
        grid=(heads,),
        in_specs=[
            pl.BlockSpec((rows, LANES), lambda h: (rb, 2 * h)),
            pl.BlockSpec((rows, LANES), lambda h: (rb, 2 * h + 1)),
            pl.BlockSpec((None, kv_lora, LANES), lambda h: (h, 0, 0)),
        ],
        out_specs=(pl.BlockSpec((None, rows, kv_lora), lambda h: (h, 0, 0)),
                   pl.BlockSpec((None, rows, LANES), lambda h: (h, 0, 0))),
        compiler_params=_cparams("parallel"),
        name="mla_qlat",
    )(q_arr, q_arr, wuk_t)


def _mla_attn_sample_kernel(ql_ref, qr_ref, c_ref, kr_ref, aug_ref, o_ref, *, heads, t, kv_lora, per_blk):
    b = pl.program_id(0)
    rows = heads * t
    ql = ql_ref[...].reshape(rows, kv_lora)
    qr = qr_ref[...].reshape(rows, LANES)
    c = c_ref[...].astype(BF)
    s_c = _dot_nt(ql, c) + _dot_nt(qr, kr_ref[...])
    cn = aug_ref[:, :kv_lora]
    s_n = _dot_nt(ql, cn) + _dot_nt(qr, aug_ref[:, kv_lora:])
    owner = lax.broadcasted_iota(jnp.int32, (1, LANES), 1) // t
    s_n = jnp.where(owner == b % per_blk, s_n, NEG)
    m = jnp.maximum(jnp.max(s_c, axis=-1, keepdims=True), jnp.max(s_n, axis=-1, keepdims=True))
    p_c = jnp.exp2(s_c - m)
    p_n = jnp.exp2(s_n - m)
    l = jnp.sum(p_c, axis=-1, keepdims=True) + jnp.sum(p_n, axis=-1, keepdims=True)
    o = (_dot(p_c.astype(BF), c) + _dot(p_n.astype(BF), cn)) / l
    o_ref[...] = o.reshape(heads, t, kv_lora).astype(BF)


def _mla_attn_sample(q_lat, q_rope, cache_c, cache_kr_pad, aug, layer, row0, streams, t):
    heads, _, kv_lora = q_lat.shape
    past = cache_c.shape[2]
    assert past % CHUNK == 0 and t <= CHUNK and LANES % t == 0 and row0 % LANES == 0
    per_blk = LANES // t
    return pl.pallas_call(
        functools.partial(_mla_attn_sample_kernel, heads=heads, t=t, kv_lora=kv_lora, per_blk=per_blk),
        out_shape=jax.ShapeDtypeStruct((heads, streams * t, kv_lora), BF),
        grid=(streams,),
        in_specs=[
            pl.BlockSpec((heads, t, kv_lora), lambda b: (0, b, 0)),
            pl.BlockSpec((heads, t, LANES), lambda b: (0, b, 0)),
            pl.BlockSpec((None, None, past, kv_lora), lambda b: (layer, b, 0, 0)),
            pl.BlockSpec((None, None, past, LANES), lambda b: (layer, b, 0, 0)),
            pl.BlockSpec((LANES, kv_lora + LANES), lambda b: (row0 // LANES + b // per_blk, 0)),
        ],
        out_specs=pl.BlockSpec((heads, t, kv_lora), lambda b: (0, b, 0)),
        compiler_params=_cparams("parallel"),
        name="mla_attn_sample",
    )(q_lat, q_rope, cache_c, cache_kr_pad, aug)


def _mla_oup_kernel(ol_ref, w_ref, o_ref):
    o_ref[...] = _dot(ol_ref[...], w_ref[...]).astype(BF)


def _mla_oup(o_lat, wuv_t):
    heads, rows, kv_lora = o_lat.shape
    dv = wuv_t.shape[2]
    return pl.pallas_call(
        _mla_oup_kernel,
        out_shape=jax.ShapeDtypeStruct((rows, heads * dv), BF),
        grid=(heads,),
        in_specs=[pl.BlockSpec((None, rows, kv_lora), lambda h: (h, 0, 0)),
                  pl.BlockSpec((None, kv_lora, dv), lambda h: (h, 0, 0))],
        out_specs=pl.BlockSpec((rows, dv), lambda h: (0, h)),
        compiler_params=_cparams("parallel"),
        name="mla_oup",
    )(o_lat, wuv_t)


def _dsa_small_kernel(x_ref, g_ref, w_ref, gk_ref, o_ref, *, idx_dim, wi_scale):
    h = _rms(x_ref[...], g_ref[...]).astype(BF)
    y = _dot(h, w_ref[...])
    lo = y[:, :LANES]
    lane = lax.broadcasted_iota(jnp.int32, lo.shape, 1)
    ms = jnp.sum(jnp.where(lane < idx_dim, lo * lo, 0.0), axis=-1, keepdims=True) / idx_dim
    o_ref[:, :LANES] = lo * lax.rsqrt(ms + NORM_EPS) * gk_ref[...]
    o_ref[:, LANES:] = y[:, LANES:] * wi_scale


def _dsa_small(x, g3, gi, w_small, gk2, tm, idx_dim, wi_scale):
    m, d = x.shape
    return pl.pallas_call(
        functools.partial(_dsa_small_kernel, idx_dim=idx_dim, wi_scale=wi_scale),
        out_shape=jax.ShapeDtypeStruct((m, 2 * LANES), F32),
        grid=(m // tm,),
        in_specs=[pl.BlockSpec((tm, d), lambda i: (i, 0)),
                  pl.BlockSpec((None, 1, d), lambda i: (gi, 0, 0)),
                  pl.BlockSpec((d, 2 * LANES), lambda i: (0, 0)),
                  pl.BlockSpec((1, LANES), lambda i: (0, 0))],
        out_specs=pl.BlockSpec((tm, 2 * LANES), lambda i: (i, 0)),
        compiler_params=_cparams("parallel"),
        name="dsa_small",
    )(x, g3, w_small, gk2)


def _bias_tab_kernel(rb_ref, o_ref, *, keys_on_lanes):
    h = pl.program_id(0)
    half = REL_BUCKETS // 2
    max_exact = half // 2
    if keys_on_lanes:
        shape = (o_ref.shape[1], LANES)
        q_axis, k_axis = 0, 1
    else:
        shape = (LANES, o_ref.shape[2])
        q_axis, k_axis = 1, 0
    qi = lax.broadcasted_iota(jnp.int32, shape, q_axis)
    kj = lax.broadcasted_iota(jnp.int32, shape, k_axis)
    far = rb_ref[half - 1, h]
    for v in range(o_ref.shape[0]):
        for side_idx, off in enumerate((v, v - 1)):
            dist = off * LANES + qi - kj
            side = jnp.where(dist < 0, half, 0)
            a = jnp.abs(dist)
            a_f = jnp.maximum(a, 1).astype(F32)
            large = max_exact + (jnp.log(a_f / max_exact) / math.log(REL_MAX_DIST / max_exact)
                                 * (half - max_exact)).astype(jnp.int32)
            large = jnp.minimum(large, half - 1)
            bucket = side + jnp.where(a < max_exact, a, large)
            bias = jnp.zeros(shape, F32)
            for bk in range(REL_BUCKETS):
                bias = jnp.where(bucket == bk, rb_ref[bk, h], bias)
            bias = (bias - far) * LOG2E
            if keys_on_lanes:
                o_ref[v, :, side_idx * LANES:(side_idx + 1) * LANES] = bias
            else:
                o_ref[v, side_idx * LANES:(side_idx + 1) * LANES, :] = bias


def _bias_tab(rel_bias, rows):
    heads = rel_bias.shape[1]
    return pl.pallas_call(
        functools.partial(_bias_tab_kernel, keys_on_lanes=True),
        out_shape=jax.ShapeDtypeStruct((3, heads, rows, KEY_TILE), F32),
        grid=(heads,),
        in_specs=[pl.BlockSpec(memory_space=pltpu.SMEM)],
        out_specs=pl.BlockSpec((3, None, rows, KEY_TILE), lambda h: (0, h, 0, 0)),
        compiler_params=_cparams("parallel"),
        name="bias_tab",
    )(rel_bias)


def _bias_tab_t(rel_bias, kv_heads, tq):
    heads = rel_bias.shape[1]
    group = heads // kv_heads
    return pl.pallas_call(
        functools.partial(_bias_tab_kernel, keys_on_lanes=False),
        out_shape=jax.ShapeDtypeStruct((3, kv_heads, KEY_TILE, group * tq), F32),
        grid=(heads,),
        in_specs=[pl.BlockSpec(memory_space=pltpu.SMEM)],
        out_specs=pl.BlockSpec((3, None, KEY_TILE, tq), lambda h: (0, h // group, 0, h % group)),
        compiler_params=_cparams("parallel"),
        name="bias_tab_t",
    )(rel_bias)


def _indexer_queries(qi_ref, qim_ref, idx_dim):
    tq = qi_ref.shape[0]
    lane = lax.broadcasted_iota(jnp.int32, (tq, LANES), 1)
    per_pair = LANES // idx_dim
    for h in range(qim_ref.shape[0]):
        pair = qi_ref[:, (h // per_pair) * LANES:(h // per_pair + 1) * LANES].astype(F32)
        sub = h % per_pair
        keep = (lane >= sub * idx_dim) & (lane < (sub + 1) * idx_dim)
        qim_ref[h] = jnp.where(keep, pair, 0.0).astype(BF)


def _indexer_keys(qim_ref, wi, kd, kpos, qchunk, s_real):
    acc = jnp.zeros((qim_ref.shape[1], kd.shape[0]), F32)
    for h in range(qim_ref.shape[0]):
        acc = acc + jnp.maximum(_dot_nt(qim_ref[h], kd), 0.0) * wi[:, h:h + 1]
    valid = ((kpos >> CHUNK_SHIFT) <= qchunk) & (kpos < s_real)
    bits = lax.bitcast_convert_type(acc, jnp.int32)
    key = bits ^ ((bits >> 31) & 0x7FFFFFFF)
    return jnp.where(valid, key, INT_MIN)


def _topk_threshold(count_ge, shape, topk):
    thr = jnp.where(count_ge(jnp.zeros(shape, jnp.int32)) >= topk, 0, INT_MIN).astype(jnp.int32)

    def bit_body(b, thr):
        cand = thr | (jnp.int32(1) << (30 - b))
        return jnp.where(count_ge(cand) >= topk, cand, thr)

    thr = lax.fori_loop(0, 31, bit_body, thr)
    return jnp.maximum(thr, INT_MIN + 1)


def _dsa_prompt_kernel(q_ref, qi_ref, wi_ref, k_ref, vt_ref, kd_ref, tab_ref, o_ref,
                       key_ref, mb_ref, qim_ref, qg_ref, m_ref, l_ref, acc_ref, *, tq, topk, kv_heads, group, idx_dim):
    i = pl.program_id(1)
    qb = i * (tq // LANES)
    n_valid = qb // 2 + 1
    n_far = jnp.maximum((qb + 1) // 2 - 1, 0)
    qchunk = (i * tq + lax.broadcasted_iota(jnp.int32, (1, tq), 1)) >> CHUNK_SHIFT
    s_real = k_ref.shape[0]
    dh = LANES
    n_pairs = qim_ref.shape[0]

    lane = lax.broadcasted_iota(jnp.int32, (tq, LANES), 1)
    for j in range(n_pairs):
        pair = qi_ref[:, j * LANES:(j + 1) * LANES].astype(F32)
        qim_ref[j, :tq] = jnp.where(lane < idx_dim, pair, 0.0).astype(BF)
        qim_ref[j, tq:] = jnp.where(lane >= idx_dim, pair, 0.0).astype(BF)
    for g in range(kv_heads):
        qg_ref[g] = jnp.concatenate(
            [q_ref[:, (group * g + r) * dh:(group * g + r + 1) * dh] for r in range(group)], axis=0)
    wi_t = wi_ref[...].T

    def tile_start(t):
        return pl.multiple_of(t * KEY_TILE, KEY_TILE)

    def idx_body(t, carry):
        ks = tile_start(t)
        kd = kd_ref[pl.ds(ks, KEY_TILE), :].astype(BF)
        acc = jnp.zeros((KEY_TILE, tq), F32)
        for j in range(n_pairs):
            sc = _dot_nt(kd, qim_ref[j])
            acc = acc + jnp.maximum(sc[:, :tq], 0.0) * wi_t[2 * j:2 * j + 1, :]
            acc = acc + jnp.maximum(sc[:, tq:], 0.0) * wi_t[2 * j + 1:2 * j + 2, :]
        kpos = ks + lax.broadcasted_iota(jnp.int32, (KEY_TILE, 1), 0)
        valid = ((kpos >> CHUNK_SHIFT) <= qchunk) & (kpos < s_real)
        bits = lax.bitcast_convert_type(acc, jnp.int32)
        key = bits ^ ((bits >> 31) & 0x7FFFFFFF)
        key_ref[pl.ds(ks, KEY_TILE), :] = jnp.where(valid, key, INT_MIN)
        return carry

    lax.fori_loop(0, n_valid, idx_body, 0)

    fold = 64

    def count_ge(cand):
        def body(t, acc):
            c = jnp.where(key_ref[pl.ds(tile_start(t), KEY_TILE), :] >= cand, 1.0, 0.0)
            return acc + jnp.sum(c.reshape(KEY_TILE // fold, fold, tq), axis=0)
        acc = lax.fori_loop(0, n_valid, body, jnp.zeros((fold, tq), F32))
        return jnp.sum(acc, axis=0, keepdims=True)

    thr = _topk_threshold(count_ge, (1, tq), topk)

    def mb_body(t, carry):
        ks = tile_start(t)
        mb_ref[pl.ds(ks, KEY_TILE), :] = jnp.where(key_ref[pl.ds(ks, KEY_TILE), :] >= thr, 0.0, NEG)
        return carry

    lax.fori_loop(0, n_valid, mb_body, 0)

    m_ref[...] = jnp.full(m_ref.shape, NEG, F32)
    l_ref[...] = jnp.zeros(l_ref.shape, F32)
    acc_ref[...] = jnp.zeros(acc_ref.shape, F32)

    def tile(t, near):
        ks = tile_start(t)
        mb = mb_ref[pl.ds(ks, KEY_TILE), :]
        mb = jnp.concatenate([mb] * group, axis=1)

        def scores(g):
            s_t = _dot_nt(k_ref[pl.ds(ks, KEY_TILE), g * dh:(g + 1) * dh], qg_ref[g]) + mb
            return s_t + tab_ref[qb - 2 * t, g] if near else s_t

        s_next = scores(0)
        for g in range(kv_heads):
            s_t = s_next
            if g + 1 < kv_heads:
                s_next = scores(g + 1)
            _online_update_t(s_t, vt_ref[g * dh:(g + 1) * dh, pl.ds(ks, KEY_TILE)], m_ref, l_ref, acc_ref, g)

    def far_body(t, carry):
        tile(t, False)
        return carry

    def near_body(t, carry):
        tile(t, True)
        return carry

    lax.fori_loop(0, n_far, far_body, 0)
    lax.fori_loop(n_far, n_valid, near_body, 0)
    for g in range(kv_heads):
        o_t = acc_ref[g] / l_ref[g]
        for r in range(group):
            h = group * g + r
            o_ref[:, h * dh:(h + 1) * dh] = o_t[:, r * tq:(r + 1) * tq].T.astype(BF)


def _dsa_sample_kernel(q_ref, qi_ref, wi_ref, kdc_ref, kdn_ref, kn_ref, vn_ref, tab_ref, kc_ref, vc_ref,
                       o_ref, key_ref, qim_ref, *, t, past, topk, kv_heads, group, idx_dim):
    dh = LANES

    def cached(ref, g, c):
        return ref[pl.ds(c * KEY_TILE * kv_heads + g, KEY_TILE, stride=kv_heads), :].astype(BF)

    qb = past // LANES
    n_c = past // KEY_TILE
    s_real = past + t
    qchunk = (past + lax.broadcasted_iota(jnp.int32, (t, 1), 0)) >> CHUNK_SHIFT

    _indexer_queries(qi_ref, qim_ref, idx_dim)
    wi = wi_ref[...]
    for c in range(n_c):
        kpos = c * KEY_TILE + lax.broadcasted_iota(jnp.int32, (t, KEY_TILE), 1)
        key_ref[:, c * KEY_TILE:(c + 1) * KEY_TILE] = _indexer_keys(
            qim_ref, wi, kdc_ref[c * KEY_TILE:(c + 1) * KEY_TILE, :], kpos, qchunk, s_real)
    kpos = past + lax.broadcasted_iota(jnp.int32, (t, LANES), 1)
    key_ref[:, past:] = _indexer_keys(qim_ref, wi, kdn_ref[...].astype(BF), kpos, qchunk, s_real)

    def count_ge(cand):
        return jnp.sum(jnp.where(key_ref[...] >= cand, 1.0, 0.0), axis=-1, keepdims=True)

    thr = _topk_threshold(count_ge, (t, 1), topk)
    mb = jnp.where(key_ref[...] >= thr, 0.0, NEG)

    for g in range(kv_heads):
        hs = slice(group * g, group * (g + 1))
        qg = jnp.concatenate(
            [q_ref[:, (group * g + r) * dh:(group * g + r + 1) * dh] for r in range(group)], axis=0)
        parts = []
        for c in range(n_c):
            s = _dot_nt(qg, cached(kc_ref, g, c)).reshape(group, t, KEY_TILE)
            v = qb - 2 * c
            if v <= 2:
                s = s + tab_ref[v, hs]
            parts.append(s)
        s = _dot_nt(qg, kn_ref[:, g * dh:(g + 1) * dh]).reshape(group, t, LANES)
        parts.append(s + tab_ref[0, hs, :, :LANES])
        s_all = jnp.concatenate(parts, axis=-1) + mb[None]
        m = jnp.max(s_all, axis=-1, keepdims=True)
        p = jnp.exp2(s_all - m)
        l = jnp.sum(p, axis=-1, keepdims=True)
        pb = p.astype(BF).reshape(group * t, past + LANES)
        acc = _dot(pb[:, past:], vn_ref[:, g * dh:(g + 1) * dh])
        for c in range(n_c):
            acc = acc + _dot(pb[:, c * KEY_TILE:(c + 1) * KEY_TILE], cached(vc_ref, g, c))
        out = acc.reshape(group, t, dh) / l
        for r in range(group):
            o_ref[:, (group * g + r) * dh:(group * g + r + 1) * dh] = out[r].astype(BF)


def _dsa_prompt(qkv, small, vt, tab_t, *, batch, seq, heads, kv_heads, idx_heads, idx_dim):
    tq = LANES
    assert seq % KEY_TILE == 0
    nt = seq // tq
    dh = LANES
    group = heads // kv_heads
    qcols, kvcols, qicols = heads * dh, kv_heads * dh, idx_heads * idx_dim
    kern = functools.partial(_dsa_prompt_kernel, tq=tq, topk=min(TOPK_MAX, seq // 4),
                             kv_heads=kv_heads, group=group, idx_dim=idx_dim)
    return pl.pallas_call(
        kern,
        out_shape=jax.ShapeDtypeStruct((batch * seq, qcols), BF),
        grid=(batch, nt),
        in_specs=[
            pl.BlockSpec((tq, qcols), lambda b, i: (b * nt + i, 0)),
            pl.BlockSpec((tq, qicols), lambda b, i: (b * nt + i, (qcols + 2 * kvcols) // qicols)),
            pl.BlockSpec((tq, LANES), lambda b, i: (b * nt + i, 1)),
            pl.BlockSpec((seq, kvcols), lambda b, i: (b, qcols // kvcols)),
            pl.BlockSpec((kvcols, seq), lambda b, i: (0, b)),
            pl.BlockSpec((seq, LANES), lambda b, i: (b, 0)),
            pl.BlockSpec((3, kv_heads, KEY_TILE, group * tq), lambda b, i: (0, 0, 0, 0)),
        ],
        out_specs=pl.BlockSpec((tq, qcols), lambda b, i: (b * nt + i, 0)),
        scratch_shapes=[pltpu.VMEM((seq, tq), jnp.int32),
                        pltpu.VMEM((seq, tq), F32),
                        pltpu.VMEM((idx_heads * idx_dim // LANES, 2 * tq, LANES), BF),
                        pltpu.VMEM((kv_heads, group * tq, dh), BF),
                        pltpu.VMEM((kv_heads, 1, group * tq), F32),
                        pltpu.VMEM((kv_heads, 1, group * tq), F32),
                        pltpu.VMEM((kv_heads, dh, group * tq), F32)],
        compiler_params=_cparams("parallel", "arbitrary"),
        name="dsa_prompt",
    )(qkv, qkv, small, qkv, vt, small, tab_t)


def _dsa_sample(qkv, small, kd_cache, kd_new, k_new, v_new, cache_k, cache_v, tab, *, layer, row0, streams, t,
                heads, kv_heads, idx_heads, idx_dim):
    n_layers, _, past = cache_k.shape[:3]
    dh = LANES
    group = heads // kv_heads
    qcols, kvcols, qicols = heads * dh, kv_heads * dh, idx_heads * idx_dim
    assert past % KEY_TILE == 0 and t <= LANES and row0 % t == 0
    rb0 = row0 // t
    kern = functools.partial(_dsa_sample_kernel, t=t, past=past, topk=min(TOPK_MAX, (past + t) // 4),
                             kv_heads=kv_heads, group=group, idx_dim=idx_dim)
    rows = lambda c: c.reshape(n_layers, streams, past * kv_heads, dh)
    cache_spec = pl.BlockSpec((None, None, past * kv_heads, dh), lambda b: (layer, b, 0, 0))

    return pl.pallas_call(
        kern,
        out_shape=jax.ShapeDtypeStruct((streams * t, qcols), BF),
        grid=(streams,),
        in_specs=[
            pl.BlockSpec((t, qcols), lambda b: (rb0 + b, 0)),
            pl.BlockSpec((t, qicols), lambda b: (rb0 + b, (qcols + 2 * kvcols) // qicols)),
            pl.BlockSpec((t, LANES), lambda b: (rb0 + b, 1)),
            pl.BlockSpec((past, LANES), lambda b: (b, 0)),
            pl.BlockSpec((LANES, LANES), lambda b: (b, 0)),
            pl.BlockSpec((LANES, kvcols), lambda b: (b, 0)),
            pl.BlockSpec((LANES, kvcols), lambda b: (b, 0)),
            pl.BlockSpec((3, heads, t, KEY_TILE), lambda b: (0, 0, 0, 0)),
            cache_spec,
            cache_spec,
        ],
        out_specs=pl.BlockSpec((t, qcols), lambda b: (b, 0)),
        scratch_shapes=[pltpu.VMEM((t, past + LANES), jnp.int32),
                        pltpu.VMEM((idx_heads, t, LANES), BF)],
        compiler_params=_cparams("parallel"),
        name="dsa_sample",
    )(qkv, qkv, small, kd_cache, kd_new, k_new, v_new, tab, rows(cache_k), rows(cache_v))


def _rope_tables(pos, rope):
    half = rope // 2
    inv_freq = ROPE_THETA ** (-jnp.arange(half, dtype=F32) / half)
    ang = pos.astype(F32)[:, None] * inv_freq[None, :]
    cos, sin = jnp.cos(ang), jnp.sin(ang)
    pad = jnp.zeros((pos.shape[0], LANES - rope), F32)
    return (jnp.concatenate([cos, cos, pad], axis=1), jnp.concatenate([sin, sin, pad], axis=1))


def _swap_halves(w):
    half = w.shape[-1] // 2
    return jnp.concatenate([-w[..., half:], w[..., :half]], axis=-1)


@jax.jit
def _step(x_prompt, x_sample, cache_mla_latent, cache_mla_krope, cache_dsa_k, cache_dsa_v, cache_dsa_kidx,
          norm_g, norm_f, ffn_w_in, ffn_w_out, mla_w_down, mla_g_q, mla_g_kv, mla_w_uq, mla_w_uk, mla_w_uv,
          mla_w_o, dsa_w_in, dsa_g_kidx, dsa_w_o, rel_bias):
    batch, seq, d_model = x_prompt.shape
    streams, dec, _ = x_sample.shape
    depth = norm_g.shape[0]
    past = cache_mla_latent.shape[2]
    kv_lora = cache_mla_latent.shape[3]
    rope = cache_mla_krope.shape[3]
    q_lora = mla_w_uq.shape[1]
    mla_heads = mla_w_uq.shape[2]
    nope = mla_w_uk.shape[3]
    kv_heads, dh = cache_dsa_k.shape[3], cache_dsa_k.shape[4]
    idx_dim = cache_dsa_kidx.shape[3]
    dsa_heads = dsa_w_o.shape[1] // dh
    idx_heads = rel_bias.shape[1]
    assert nope == LANES and dh == LANES and 2 * rope == LANES and mla_w_uv.shape[3] == LANES
    assert 2 * idx_dim == LANES and idx_heads <= LANES
    mp, ms = batch * seq, streams * dec
    m = mp + ms
    tm = m // 8
    assert tm * 8 == m and tm % 16 == 0
    qcols, kvcols, qicols = dsa_heads * dh, kv_heads * dh, idx_heads * idx_dim
    big = qcols + 2 * kvcols + qicols

    x = jnp.concatenate([x_prompt.reshape(mp, d_model), x_sample.reshape(ms, d_model)], axis=0)
    g3 = norm_g.reshape(depth * 3, 1, d_model)
    pos = jnp.concatenate([jnp.tile(jnp.arange(seq, dtype=jnp.int32), batch),
                           jnp.tile(past + jnp.arange(dec, dtype=jnp.int32), streams)])
    cos_t, sin_t = _rope_tables(pos, rope)
    tab = _bias_tab(rel_bias, dec)
    tab_t = _bias_tab_t(rel_bias, kv_heads, LANES)

    def ffn(x, li, s):
        act = _ffn_in(x, g3, li * 3 + 2 * s, ffn_w_in, (li, s), tm, 512)
        return _mm_res(act, ffn_w_out, (li, s), x, 0.5, tm, 256)

    def new_slab(a):
        a = a.reshape(streams, dec, a.shape[-1])
        return jnp.pad(a, ((0, 0), (0, LANES - dec), (0, 0))).reshape(streams * LANES, a.shape[-1])

    new_c, new_kr, new_k, new_v, new_ki = [], [], [], [], []
    for li in range(depth):
        j = li // 2
        gi = li * 3 + 1
        x = ffn(x, li, 0)
        if li % 2 == 0:
            w_kr = mla_w_down[j, :, q_lora + kv_lora:]
            w_kr2 = jnp.concatenate([w_kr, _swap_halves(w_kr)], axis=1)[None]
            d = _norm_mm(x, g3, gi, mla_w_down, (j,), 0, (q_lora + kv_lora) // 512, 512, F32, tm)
            kr2 = _norm_mm(x, g3, gi, w_kr2, (0,), 0, 1, LANES, F32, tm)
            cq, ckv, aug, kr = _mla_post(d, kr2, mla_g_q[j], mla_g_kv[j], cos_t, sin_t, tm // 2,
                                         q_lora, kv_lora, rope)
            new_c.append(ckv)
            new_kr.append(kr)
            uq = mla_w_uq[j]
            uq_rope = uq[..., nope:]
            wq = jnp.concatenate([uq[..., :nope], uq_rope, _swap_halves(uq_rope)], axis=-1)
            wq = jnp.transpose(wq, (1, 0, 2)).astype(BF)
            scale = (nope + rope) ** -0.5 * LOG2E
            q_arr = _mla_qup(cq, wq, cos_t, sin_t, tm, rope, scale)
            w_kexp = jnp.zeros((kv_lora + LANES, mla_heads, 2 * LANES), F32)
            w_kexp = w_kexp.at[:kv_lora, :, :nope].set(mla_w_uk[j])
            w_kexp = w_kexp.at[kv_lora:kv_lora + rope, :, nope:nope + rope].set(
                jnp.broadcast_to(jnp.eye(rope, dtype=F32)[:, None, :], (rope, mla_heads, rope)))
            w_kexp = w_kexp.reshape(kv_lora + LANES, mla_heads * 2 * LANES).astype(BF)
            k_arr = _mm(aug, w_kexp, mp, kv_lora + LANES, 1024, 512, BF)
            wuv_rows = jnp.transpose(mla_w_uv[j], (1, 2, 0)).reshape(mla_heads * LANES, kv_lora).astype(BF)
            vt_arr = _mm_nt(wuv_rows, aug, mp, kv_lora, 512, 1024, BF)
            o_p = _mla_attn_prompt(q_arr, k_arr, vt_arr, batch, seq, mla_heads, 512, 4)
            wuk_t = jnp.transpose(mla_w_uk[j], (1, 0, 2)).astype(BF)
            wuv_t = jnp.transpose(mla_w_uv[j], (1, 0, 2)).astype(BF)
            q_lat, q_rope = _mla_qlat(q_arr, wuk_t, mp, ms)
            kr_pad = jnp.pad(cache_mla_krope.astype(BF), ((0, 0), (0, 0), (0, 0), (0, LANES - rope)))
            o_lat = _mla_attn_sample(q_lat, q_rope, cache_mla_latent, kr_pad, aug, j, mp, streams, dec)
            o_s = _mla_oup(o_lat, wuv_t)
            o = jnp.concatenate([o_p, o_s], axis=0)
            x = _mm_res(o, mla_w_o, (j,), x, 1.0, tm, 256)
        else:
            qkv = _norm_mm(x, g3, gi, dsa_w_in, (j,), 0, big // 512, 512, BF, tm,
                           scale_blocks=(qcols // 512, dh ** -0.5 * LOG2E))
            kv32 = _norm_mm(x, g3, gi, dsa_w_in, (j,), qcols // 512, 2 * kvcols // 512, 512, F32, tm)
            w_ki = dsa_w_in[j, :, big:big + idx_dim]
            w_wi = dsa_w_in[j, :, big + idx_dim:]
            w_small = jnp.concatenate(
                [w_ki, w_ki, w_wi, jnp.zeros((d_model, LANES - idx_heads), F32)], axis=1).astype(BF)
            gk2 = jnp.tile(dsa_g_kidx[j], 2).reshape(1, LANES)
            small = _dsa_small(x, g3, gi, w_small, gk2, tm, idx_dim, (idx_dim ** -0.5) * (idx_heads ** -0.5))
            new_k.append(kv32[:, :kvcols])
            new_v.append(kv32[:, kvcols:])
            new_ki.append(small[:, :idx_dim])
            common = dict(heads=dsa_heads, kv_heads=kv_heads, idx_heads=idx_heads, idx_dim=idx_dim)
            vt = qkv[:mp, qcols + kvcols:qcols + 2 * kvcols].T
            o_p = _dsa_prompt(qkv, small, vt, tab_t, batch=batch, seq=seq, **common)
            ki_c = cache_dsa_kidx[j].astype(BF).reshape(streams * past, idx_dim)
            o_s = _dsa_sample(qkv, small, jnp.concatenate([ki_c, ki_c], axis=-1), new_slab(small[mp:, :LANES]),
                              new_slab(qkv[mp:, qcols:qcols + kvcols]),
                              new_slab(qkv[mp:, qcols + kvcols:qcols + 2 * kvcols]),
                              cache_dsa_k, cache_dsa_v, tab, layer=j, row0=mp, streams=streams, t=dec, **common)
            o = jnp.concatenate([o_p, o_s], axis=0)
            x = _mm_res(o, dsa_w_o, (j,), x, 1.0, tm, 256)
        x = ffn(x, li, 1)

    y = _final_norm(x, norm_f, tm // 2)

    def split(rows, *tail):
        a = jnp.stack(rows)
        n = a.shape[0]
        return (a[:, :mp].reshape(n, batch, seq, *tail), a[:, mp:].reshape(n, streams, dec, *tail))

    p_c, s_c = split(new_c, kv_lora)
    p_kr, s_kr = split(new_kr, rope)
    p_k, s_k = split(new_k, kv_heads, dh)
    p_v, s_v = split(new_v, kv_heads, dh)
    p_ki, s_ki = split(new_ki, idx_dim)
    return (y[:mp].reshape(batch, seq, d_model), y[mp:].reshape(streams, dec, d_model),
            p_c, p_kr, p_k, p_v, p_ki, s_c, s_kr, s_k, s_v, s_ki)


def kernel(x_prompt, x_sample, cache_mla_latent, cache_mla_krope, cache_dsa_k, cache_dsa_v, cache_dsa_kidx,
           norm_g, norm_f, ffn_w_in, ffn_w_out, mla_w_down, mla_g_q, mla_g_kv, mla_w_uq, mla_w_uk, mla_w_uv,
           mla_w_o, dsa_w_in, dsa_g_kidx, dsa_w_o, rel_bias):
    return _step(x_prompt, x_sample, cache_mla_latent, cache_mla_krope, cache_dsa_k, cache_dsa_v,
                 cache_dsa_kidx, norm_g, norm_f, ffn_w_in, ffn_w_out, mla_w_down, mla_g_q, mla_g_kv,
                 mla_w_uq, mla_w_uk, mla_w_uv, mla_w_o, dsa_w_in, dsa_g_kidx, dsa_w_o, rel_bias)
```

```python
import functools
import math

import jax
import jax.numpy as jnp
from jax import lax
from jax.experimental import pallas as pl
from jax.experimental.pallas import tpu as pltpu

BF = jnp.bfloat16
F32 = jnp.float32

CHUNK = 64
ROPE_THETA = 10000.0
NORM_EPS = 1e-6
TOPK_MAX = 256
REL_BUCKETS = 32
REL_MAX_DIST = 128

LANES = 128
KEY_TILE = 2 * LANES
VMEM_LIMIT = 56 * 1024 * 1024
NEG = -1e30
INT_MIN = -(2 ** 31)
LOG2E = math.log2(math.e)
CHUNK_SHIFT = CHUNK.bit_length() - 1
assert 1 << CHUNK_SHIFT == CHUNK

NT = (((1,), (1,)), ((), ()))


def _cparams(*sem):
    return pltpu.CompilerParams(dimension_semantics=sem, vmem_limit_bytes=VMEM_LIMIT)


def _rms(x, g):
    ms = jnp.mean(x * x, axis=-1, keepdims=True)
    return x * lax.rsqrt(ms + NORM_EPS) * g


def _dot(a, b):
    return jnp.dot(a, b, preferred_element_type=F32)


def _dot_nt(a, b):
    return lax.dot_general(a, b, NT, preferred_element_type=F32)


def _wspec(prefix, k, tn, col0=0):
    return pl.BlockSpec((None,) * len(prefix) + (k, tn),
                        lambda i, j: tuple(prefix) + (0, col0 + j))


def _ffn_in_kernel(x_ref, g_ref, wg_ref, wu_ref, o_ref, h_ref):
    @pl.when(pl.program_id(1) == 0)
    def _():
        h_ref[...] = _rms(x_ref[...], g_ref[...]).astype(BF)

    h = h_ref[...]
    gate = _dot(h, wg_ref[...].astype(BF))
    up = _dot(h, wu_ref[...].astype(BF))
    o_ref[...] = (gate / (1.0 + jnp.exp(-gate)) * up).astype(BF)


def _ffn_in(x, g3, gi, w_in, prefix, tm, tf):
    m, d = x.shape
    d_ff = w_in.shape[-1] // 2
    nf = d_ff // tf
    return pl.pallas_call(
        _ffn_in_kernel,
        out_shape=jax.ShapeDtypeStruct((m, d_ff), BF),
        grid=(m // tm, nf),
        in_specs=[
            pl.BlockSpec((tm, d), lambda i, j: (i, 0), pipeline_mode=pl.Buffered(1)),
            pl.BlockSpec((None, 1, d), lambda i, j: (gi, 0, 0)),
            _wspec(prefix, d, tf),
            _wspec(prefix, d, tf, nf),
        ],
        out_specs=pl.BlockSpec((tm, tf), lambda i, j: (i, j)),
        scratch_shapes=[pltpu.VMEM((tm, d), BF)],
        compiler_params=_cparams("parallel", "arbitrary"),
        name="ffn_in",
    )(x, g3, w_in, w_in)


def _mm_res_kernel(a_ref, w_ref, x_ref, o_ref, *, scale):
    acc = _dot(a_ref[...], w_ref[...].astype(BF))
    o_ref[...] = x_ref[...] + scale * acc


def _mm_res(a, w, prefix, x, scale, tm, tn):
    m, k = a.shape
    n = x.shape[1]
    return pl.pallas_call(
        functools.partial(_mm_res_kernel, scale=scale),
        out_shape=jax.ShapeDtypeStruct((m, n), F32),
        grid=(m // tm, n // tn),
        in_specs=[
            pl.BlockSpec((tm, k), lambda i, j: (i, 0)),
            _wspec(prefix, k, tn),
            pl.BlockSpec((tm, tn), lambda i, j: (i, j)),
        ],
        out_specs=pl.BlockSpec((tm, tn), lambda i, j: (i, j)),
        compiler_params=_cparams("parallel", "arbitrary"),
        name="mm_res",
    )(a, w, x)


def _norm_mm_kernel(x_ref, g_ref, w_ref, o_ref, *rest, scale_blocks, keep_blk):
    h_ref = rest[-1]
    j = pl.program_id(1)

    @pl.when(j == 0)
    def _():
        h_ref[...] = _rms(x_ref[...], g_ref[...]).astype(BF)

    acc = _dot(h_ref[...], w_ref[...].astype(BF))
    if keep_blk is not None:
        @pl.when(j == keep_blk)
        def _():
            rest[0][...] = acc
    if scale_blocks is not None:
        nb, val = scale_blocks
        acc = acc * jnp.where(j < nb, val, 1.0).astype(F32)
    o_ref[...] = acc.astype(o_ref.dtype)


def _norm_mm(x, g3, gi, w, prefix, n_blk, tn, out_dtype, tm, scale_blocks=None, keep_blk=None):
    m, d = x.shape
    out_shape = [jax.ShapeDtypeStruct((m, n_blk * tn), out_dtype)]
    out_specs = [pl.BlockSpec((tm, tn), lambda i, j: (i, j))]
    if keep_blk is not None:
        out_shape.append(jax.ShapeDtypeStruct((m, tn), F32))
        out_specs.append(pl.BlockSpec((tm, tn), lambda i, j: (i, 0)))
    out = pl.pallas_call(
        functools.partial(_norm_mm_kernel, scale_blocks=scale_blocks, keep_blk=keep_blk),
        out_shape=out_shape,
        grid=(m // tm, n_blk),
        in_specs=[
            pl.BlockSpec((tm, d), lambda i, j: (i, 0), pipeline_mode=pl.Buffered(1)),
            pl.BlockSpec((None, 1, d), lambda i, j: (gi, 0, 0)),
            _wspec(prefix, d, tn),
        ],
        out_specs=out_specs,
        scratch_shapes=[pltpu.VMEM((tm, d), BF)],
        compiler_params=_cparams("parallel", "arbitrary"),
        name="norm_mm",
    )(x, g3, w)
    return out if keep_blk is not None else out[0]


def _mm_kernel(a_ref, w_ref, o_ref):
    o_ref[...] = _dot(a_ref[...], w_ref[...].astype(BF)).astype(o_ref.dtype)


def _mm(a, w, m_rows, k_cols, tm, tn, out_dtype):
    n = w.shape[1]
    return pl.pallas_call(
        _mm_kernel,
        out_shape=jax.ShapeDtypeStruct((m_rows, n), out_dtype),
        grid=(m_rows // tm, n // tn),
        in_specs=[
            pl.BlockSpec((tm, k_cols), lambda i, j: (i, 0)),
            pl.BlockSpec((k_cols, tn), lambda i, j: (0, j)),
        ],
        out_specs=pl.BlockSpec((tm, tn), lambda i, j: (i, j)),
        compiler_params=_cparams("parallel", "arbitrary"),
        name="mm",
    )(a, w)


def _mm_nt_kernel(w_ref, a_ref, o_ref):
    o_ref[...] = _dot_nt(w_ref[...], a_ref[...]).astype(o_ref.dtype)


def _mm_nt(w_t, a, m_rows, k_cols, tn_rows, tm_cols, out_dtype):
    n = w_t.shape[0]
    return pl.pallas_call(
        _mm_nt_kernel,
        out_shape=jax.ShapeDtypeStruct((n, m_rows), out_dtype),
        grid=(n // tn_rows, m_rows // tm_cols),
        in_specs=[
            pl.BlockSpec((tn_rows, k_cols), lambda i, j: (i, 0)),
            pl.BlockSpec((tm_cols, k_cols), lambda i, j: (j, 0)),
        ],
        out_specs=pl.BlockSpec((tn_rows, tm_cols), lambda i, j: (i, j)),
        compiler_params=_cparams("parallel", "arbitrary"),
        name="mm_nt",
    )(w_t, a)


def _final_norm_kernel(x_ref, g_ref, o_ref):
    o_ref[...] = _rms(x_ref[...], g_ref[...])


def _final_norm(x, g, row0, rows, tm):
    d = x.shape[1]
    assert row0 % tm == 0 and rows % tm == 0
    return pl.pallas_call(
        _final_norm_kernel,
        out_shape=jax.ShapeDtypeStruct((rows, d), F32),
        grid=(rows // tm,),
        in_specs=[pl.BlockSpec((tm, d), lambda i: (row0 // tm + i, 0)),
                  pl.BlockSpec((1, d), lambda i: (0, 0))],
        out_specs=pl.BlockSpec((tm, d), lambda i: (i, 0)),
        compiler_params=_cparams("parallel"),
        name="final_norm",
    )(x, g.reshape(1, d))


def _mla_post_kernel(d_ref, kr2_ref, gq_ref, gkv_ref, c_ref, s_ref,
                     cq_ref, ckv_ref, aug_ref, kr_ref, *, q_lora, kv_lora, rope):
    d = d_ref[...]
    cq_ref[...] = _rms(d[:, :q_lora], gq_ref[...]).astype(BF)
    ckv = _rms(d[:, q_lora:q_lora + kv_lora], gkv_ref[...])
    ckv_ref[...] = ckv
    k2 = kr2_ref[...]
    kr = k2 * c_ref[...] + pltpu.roll(k2, rope, 1) * s_ref[...]
    aug_ref[:, :kv_lora] = ckv.astype(BF)
    aug_ref[:, kv_lora:] = kr.astype(BF)
    kr_ref[...] = kr[:, :rope]


def _mla_post(d, kr2, g_q, g_kv, cos_t, sin_t, tm, q_lora, kv_lora, rope):
    m = d.shape[0]
    row = lambda w: pl.BlockSpec((tm, w), lambda i: (i, 0))
    vec = lambda w: pl.BlockSpec((1, w), lambda i: (0, 0))
    return pl.pallas_call(
        functools.partial(_mla_post_kernel, q_lora=q_lora, kv_lora=kv_lora, rope=rope),
        out_shape=(jax.ShapeDtypeStruct((m, q_lora), BF),
                   jax.ShapeDtypeStruct((m, kv_lora), F32),
                   jax.ShapeDtypeStruct((m, kv_lora + LANES), BF),
                   jax.ShapeDtypeStruct((m, rope), F32)),
        grid=(m // tm,),
        in_specs=[row(d.shape[1]), row(LANES), vec(q_lora), vec(kv_lora), row(LANES), row(LANES)],
        out_specs=(row(q_lora), row(kv_lora), row(kv_lora + LANES), row(rope)),
        compiler_params=_cparams("parallel"),
        name="mla_post",
    )(d, kr2, g_q.reshape(1, -1), g_kv.reshape(1, -1), cos_t, sin_t)


def _mla_qup_kernel(cq_ref, w_ref, c_ref, s_ref, o_ref, *, rope, scale, hq):
    cq = cq_ref[...]
    for h in range(hq):
        acc = _dot(cq, w_ref[h])
        lo = acc[:, :LANES]
        up = acc[:, LANES:]
        rot = up * c_ref[...] + pltpu.roll(up, rope, 1) * s_ref[...]
        o_ref[:, 2 * h * LANES:(2 * h + 1) * LANES] = (lo * scale).astype(BF)
        o_ref[:, (2 * h + 1) * LANES:(2 * h + 2) * LANES] = (rot * scale).astype(BF)


def _mla_qup(cq, wq, cos_t, sin_t, tm, rope, scale, hq):
    m, k = cq.shape
    heads = wq.shape[0]
    return pl.pallas_call(
        functools.partial(_mla_qup_kernel, rope=rope, scale=scale, hq=hq),
        out_shape=jax.ShapeDtypeStruct((m, heads * 2 * LANES), BF),
        grid=(m // tm, heads // hq),
        in_specs=[
            pl.BlockSpec((tm, k), lambda i, h: (i, 0)),
            pl.BlockSpec((hq, k, 2 * LANES), lambda i, h: (h, 0, 0)),
            pl.BlockSpec((tm, LANES), lambda i, h: (i, 0)),
            pl.BlockSpec((tm, LANES), lambda i, h: (i, 0)),
        ],
        out_specs=pl.BlockSpec((tm, hq * 2 * LANES), lambda i, h: (i, h)),
        compiler_params=_cparams("parallel", "arbitrary"),
        name="mla_qup",
    )(cq, wq, cos_t, sin_t)


def _online_update(s, v_tile, m_ref, l_ref, acc_ref, hs):
    lead = s.shape[:-1]
    m_old = m_ref[hs]
    m_new = jnp.maximum(m_old, jnp.max(s, axis=-1, keepdims=True))
    alpha = jnp.exp2(m_old - m_new)
    p = jnp.exp2(s - m_new)
    l_ref[hs] = alpha * l_ref[hs] + jnp.sum(p, axis=-1, keepdims=True)
    pv = _dot(p.reshape(-1, s.shape[-1]).astype(BF), v_tile)
    acc_ref[hs] = alpha * acc_ref[hs] + pv.reshape(lead + (v_tile.shape[1],))
    m_ref[hs] = m_new


def _online_update_t(s_t, vt_tile, m_ref, l_ref, acc_ref, hs):
    m_old = m_ref[hs]
    m_new = jnp.maximum(m_old, jnp.max(s_t, axis=0, keepdims=True))
    alpha = jnp.exp2(m_old - m_new)
    p = jnp.exp2(s_t - m_new)
    l_ref[hs] = alpha * l_ref[hs] + jnp.sum(p, axis=0, keepdims=True)
    acc_ref[hs] = alpha * acc_ref[hs] + _dot(vt_tile, p.astype(BF))
    m_ref[hs] = m_new


def _mla_attn_prompt_kernel(q_ref, k_ref, vt_ref, o_ref, m_ref, l_ref, acc_ref, *, tq, hp):
    i = pl.program_id(2)
    dk, dv = 2 * LANES, LANES
    m_ref[...] = jnp.full(m_ref.shape, NEG, F32)
    l_ref[...] = jnp.zeros(l_ref.shape, F32)
    acc_ref[...] = jnp.zeros(acc_ref.shape, F32)

    def tile(t, diagonal):
        ks = pl.multiple_of(t * tq, tq)

        def scores(h):
            s_t = _dot_nt(k_ref[pl.ds(ks, tq), h * dk:(h + 1) * dk], q_ref[:, h * dk:(h + 1) * dk])
            if diagonal:
                kc = lax.broadcasted_iota(jnp.int32, (tq, 1), 0) >> CHUNK_SHIFT
                qc = lax.broadcasted_iota(jnp.int32, (1, tq), 1) >> CHUNK_SHIFT
                s_t = jnp.where(kc <= qc, s_t, NEG)
            return s_t

        s_next = scores(0)
        for h in range(hp):
            s_t = s_next
            if h + 1 < hp:
                s_next = scores(h + 1)
            _online_update_t(s_t, vt_ref[h * dv:(h + 1) * dv, pl.ds(ks, tq)], m_ref, l_ref, acc_ref, h)

    def body(t, carry):
        tile(t, False)
        return carry

    lax.fori_loop(0, i, body, 0)
    tile(i, True)
    for h in range(hp):
        o_ref[:, h * dv:(h + 1) * dv] = (acc_ref[h] / l_ref[h]).T.astype(BF)


def _mla_attn_prompt(q_arr, k_arr, vt_arr, batch, seq, heads, tq, hp, out_rows):
    assert tq % CHUNK == 0 and seq % tq == 0 and heads % hp == 0
    nt = seq // tq
    dk, dv = 2 * LANES, LANES
    return pl.pallas_call(
        functools.partial(_mla_attn_prompt_kernel, tq=tq, hp=hp),
        out_shape=jax.ShapeDtypeStruct((out_rows, heads * dv), BF),
        grid=(batch, heads // hp, nt),
        in_specs=[
            pl.BlockSpec((tq, hp * dk), lambda b, h, i: (b * nt + i, h)),
            pl.BlockSpec((seq, hp * dk), lambda b, h, i: (b, h)),
            pl.BlockSpec((hp * dv, seq), lambda b, h, i: (h, b)),
        ],
        out_specs=pl.BlockSpec((tq, hp * dv), lambda b, h, i: (b * nt + i, h)),
        scratch_shapes=[pltpu.VMEM((hp, 1, tq), F32), pltpu.VMEM((hp, 1, tq), F32),
                        pltpu.VMEM((hp, dv, tq), F32)],
        compiler_params=_cparams("parallel", "parallel", "arbitrary"),
        name="mla_attn_prompt",
    )(q_arr, k_arr, vt_arr)


def _mla_qlat_kernel(qn_ref, qr_ref, wuk_ref, ql_ref, qro_ref):
    ql_ref[...] = _dot_nt(qn_ref[...], wuk_ref[...]).astype(BF)
    qro_ref[...] = qr_ref[...]


def _mla_qlat(q_arr, wuk_t, row0, rows):
    heads, kv_lora, _ = wuk_t.shape
    rb = row0 // rows
    return pl.pallas_call(
        _mla_qlat_kernel,
        out_shape=(jax.ShapeDtypeStruct((heads, rows, kv_lora), BF),
                   jax.ShapeDtypeStruct((heads, rows, LANES), BF)),
        grid=(heads,),
        in_specs=[
            pl.BlockSpec((rows, LANES), lambda h: (rb, 2 * h)),
            pl.BlockSpec((rows, LANES), lambda h: (rb, 2 * h + 1)),
            pl.BlockSpec((None, kv_lora, LANES), lambda h: (h, 0, 0)),
        ],
        out_specs=(pl.BlockSpec((None, rows, kv_lora), lambda h: (h, 0, 0)),
                   pl.BlockSpec((None, rows, LANES), lambda h: (h, 0, 0))),
        compiler_params=_cparams("parallel"),
        name="mla_qlat",
    )(q_arr, q_arr, wuk_t)


def _mla_attn_sample_kernel(ql_ref, qr_ref, c_ref, kr_ref, aug_ref, o_ref, *, heads, t, kv_lora, per_blk):
    b = pl.program_id(0)
    rows = heads * t
    ql = ql_ref[...].reshape(rows, kv_lora)
    qr = qr_ref[...].reshape(rows, LANES)
    c = c_ref[...].astype(BF)
    s_c = _dot_nt(ql, c) + _dot_nt(qr, kr_ref[...])
    cn = aug_ref[:, :kv_lora]
    s_n = _dot_nt(ql, cn) + _dot_nt(qr, aug_ref[:, kv_lora:])
    owner = lax.broadcasted_iota(jnp.int32, (1, LANES), 1) // t
    s_n = jnp.where(owner == b % per_blk, s_n, NEG)
    m = jnp.maximum(jnp.max(s_c, axis=-1, keepdims=True), jnp.max(s_n, axis=-1, keepdims=True))
    p_c = jnp.exp2(s_c - m)
    p_n = jnp.exp2(s_n - m)
    l = jnp.sum(p_c, axis=-1, keepdims=True) + jnp.sum(p_n, axis=-1, keepdims=True)
    o = (_dot(p_c.astype(BF), c) + _dot(p_n.astype(BF), cn)) / l
    o_ref[...] = o.reshape(heads, t, kv_lora).astype(BF)


def _mla_attn_sample(q_lat, q_rope, cache_c, cache_kr_pad, aug, layer, row0, streams, t):
    heads, _, kv_lora = q_lat.shape
    past = cache_c.shape[2]
    assert past % CHUNK == 0 and t <= CHUNK and LANES % t == 0 and row0 % LANES == 0
    per_blk = LANES // t
    return pl.pallas_call(
        functools.partial(_mla_attn_sample_kernel, heads=heads, t=t, kv_lora=kv_lora, per_blk=per_blk),
        out_shape=jax.ShapeDtypeStruct((heads, streams * t, kv_lora), BF),
        grid=(streams,),
        in_specs=[
            pl.BlockSpec((heads, t, kv_lora), lambda b: (0, b, 0)),
            pl.BlockSpec((heads, t, LANES), lambda b: (0, b, 0)),
            pl.BlockSpec((None, None, past, kv_lora), lambda b: (layer, b, 0, 0)),
            pl.BlockSpec((None, None, past, LANES), lambda b: (layer, b, 0, 0)),
            pl.BlockSpec((LANES, kv_lora + LANES), lambda b: (row0 // LANES + b // per_blk, 0)),
        ],
        out_specs=pl.BlockSpec((heads, t, kv_lora), lambda b: (0, b, 0)),
        compiler_params=_cparams("parallel"),
        name="mla_attn_sample",
    )(q_lat, q_rope, cache_c, cache_kr_pad, aug)


def _mla_oup_kernel(ol_ref, w_ref, dst_ref, o_ref):
    del dst_ref
    o_ref[...] = _dot(ol_ref[...], w_ref[...]).astype(BF)


def _mla_oup(o_lat, wuv_t, dst, row0):
    heads, rows, kv_lora = o_lat.shape
    dv = wuv_t.shape[2]
    assert row0 % rows == 0
    return pl.pallas_call(
        _mla_oup_kernel,
        out_shape=jax.ShapeDtypeStruct(dst.shape, dst.dtype),
        grid=(heads,),
        in_specs=[pl.BlockSpec((None, rows, kv_lora), lambda h: (h, 0, 0)),
                  pl.BlockSpec((None, kv_lora, dv), lambda h: (h, 0, 0)),
                  pl.BlockSpec(memory_space=pl.ANY)],
        out_specs=pl.BlockSpec((rows, dv), lambda h: (row0 // rows, h)),
        input_output_aliases={2: 0},
        compiler_params=_cparams("parallel"),
        name="mla_oup",
    )(o_lat, wuv_t, dst)


def _dsa_small_kernel(x_ref, g_ref, w_ref, gk_ref, o_ref, *, idx_dim, wi_scale):
    h = _rms(x_ref[...], g_ref[...]).astype(BF)
    y = _dot(h, w_ref[...].astype(BF))
    lo = y[:, :LANES]
    lane = lax.broadcasted_iota(jnp.int32, lo.shape, 1)
    ms = jnp.sum(jnp.where(lane < idx_dim, lo * lo, 0.0), axis=-1, keepdims=True) / idx_dim
    o_ref[:, :LANES] = lo * lax.rsqrt(ms + NORM_EPS) * gk_ref[...]
    o_ref[:, LANES:] = y[:, LANES:] * wi_scale


def _dsa_small(x, g3, gi, w_small, gk2, tm, idx_dim, wi_scale):
    m, d = x.shape
    return pl.pallas_call(
        functools.partial(_dsa_small_kernel, idx_dim=idx_dim, wi_scale=wi_scale),
        out_shape=jax.ShapeDtypeStruct((m, 2 * LANES), F32),
        grid=(m // tm,),
        in_specs=[pl.BlockSpec((tm, d), lambda i: (i, 0)),
                  pl.BlockSpec((None, 1, d), lambda i: (gi, 0, 0)),
                  pl.BlockSpec((d, 2 * LANES), lambda i: (0, 0)),
                  pl.BlockSpec((1, LANES), lambda i: (0, 0))],
        out_specs=pl.BlockSpec((tm, 2 * LANES), lambda i: (i, 0)),
        compiler_params=_cparams("parallel"),
        name="dsa_small",
    )(x, g3, w_small, gk2)


def _bias_tab_kernel(rb_ref, o_ref, *, keys_on_lanes):
    h = pl.program_id(0)
    half = REL_BUCKETS // 2
    max_exact = half // 2
    if keys_on_lanes:
        shape = (o_ref.shape[1], LANES)
        q_axis, k_axis = 0, 1
    else:
        shape = (LANES, o_ref.shape[2])
        q_axis, k_axis = 1, 0
    qi = lax.broadcasted_iota(jnp.int32, shape, q_axis)
    kj = lax.broadcasted_iota(jnp.int32, shape, k_axis)
    far = rb_ref[half - 1, h]
    for v in range(o_ref.shape[0]):
        for side_idx, off in enumerate((v, v - 1)):
            dist = off * LANES + qi - kj
            side = jnp.where(dist < 0, half, 0)
            a = jnp.abs(dist)
            a_f = jnp.maximum(a, 1).astype(F32)
            large = max_exact + (jnp.log(a_f / max_exact) / math.log(REL_MAX_DIST / max_exact)
                                 * (half - max_exact)).astype(jnp.int32)
            large = jnp.minimum(large, half - 1)
            bucket = side + jnp.where(a < max_exact, a, large)
            bias = jnp.zeros(shape, F32)
            for bk in range(REL_BUCKETS):
                bias = jnp.where(bucket == bk, rb_ref[bk, h], bias)
            bias = (bias - far) * LOG2E
            if keys_on_lanes:
                o_ref[v, :, side_idx * LANES:(side_idx + 1) * LANES] = bias
            else:
                o_ref[v, side_idx * LANES:(side_idx + 1) * LANES, :] = bias


def _bias_tab(rel_bias, rows):
    heads = rel_bias.shape[1]
    return pl.pallas_call(
        functools.partial(_bias_tab_kernel, keys_on_lanes=True),
        out_shape=jax.ShapeDtypeStruct((3, heads, rows, KEY_TILE), F32),
        grid=(heads,),
        in_specs=[pl.BlockSpec(memory_space=pltpu.SMEM)],
        out_specs=pl.BlockSpec((3, None, rows, KEY_TILE), lambda h: (0, h, 0, 0)),
        compiler_params=_cparams("parallel"),
        name="bias_tab",
    )(rel_bias)


def _bias_tab_t(rel_bias, kv_heads, tq):
    heads = rel_bias.shape[1]
    group = heads // kv_heads
    return pl.pallas_call(
        functools.partial(_bias_tab_kernel, keys_on_lanes=False),
        out_shape=jax.ShapeDtypeStruct((3, kv_heads, KEY_TILE, group * tq), F32),
        grid=(heads,),
        in_specs=[pl.BlockSpec(memory_space=pltpu.SMEM)],
        out_specs=pl.BlockSpec((3, None, KEY_TILE, tq), lambda h: (0, h // group, 0, h % group)),
        compiler_params=_cparams("parallel"),
        name="bias_tab_t",
    )(rel_bias)


def _topk_threshold(count_ge, shape, topk):
    thr = jnp.where(count_ge(jnp.zeros(shape, jnp.int32)) >= topk, 0, INT_MIN).astype(jnp.int32)

    def bit_body(b, thr):
        cand = thr | (jnp.int32(1) << (30 - b))
        return jnp.where(count_ge(cand) >= topk, cand, thr)

    thr = lax.fori_loop(0, 31, bit_body, thr)
    return jnp.maximum(thr, INT_MIN + 1)


def _dsa_prompt_kernel(q_ref, qi_ref, wi_ref, k_ref, vt_ref, kd_ref, tab_ref, o_ref,
                       key_ref, mb_ref, qim_ref, qg_ref, m_ref, l_ref, acc_ref, *, tq, topk, kv_heads, group, idx_dim):
    i = pl.program_id(1)
    qb = i * (tq // LANES)
    n_valid = qb // 2 + 1
    n_far = jnp.maximum((qb + 1) // 2 - 1, 0)
    qchunk = (i * tq + lax.broadcasted_iota(jnp.int32, (1, tq), 1)) >> CHUNK_SHIFT
    s_real = k_ref.shape[0]
    dh = LANES
    n_pairs = qim_ref.shape[0]

    lane = lax.broadcasted_iota(jnp.int32, (tq, LANES), 1)
    for j in range(n_pairs):
        pair = qi_ref[:, j * LANES:(j + 1) * LANES].astype(F32)
        qim_ref[j, :tq] = jnp.where(lane < idx_dim, pair, 0.0).astype(BF)
        qim_ref[j, tq:] = jnp.where(lane >= idx_dim, pair, 0.0).astype(BF)
    for g in range(kv_heads):
        qg_ref[g] = jnp.concatenate(
            [q_ref[:, (group * g + r) * dh:(group * g + r + 1) * dh] for r in range(group)], axis=0)
    wi_t = wi_ref[...].T

    def tile_start(t):
        return pl.multiple_of(t * KEY_TILE, KEY_TILE)

    def idx_body(t, carry):
        ks = tile_start(t)
        kd = kd_ref[pl.ds(ks, KEY_TILE), :].astype(BF)
        acc = jnp.zeros((KEY_TILE, tq), F32)
        for j in range(n_pairs):
            sc = _dot_nt(kd, qim_ref[j])
            acc = acc + jnp.maximum(sc[:, :tq], 0.0) * wi_t[2 * j:2 * j + 1, :]
            acc = acc + jnp.maximum(sc[:, tq:], 0.0) * wi_t[2 * j + 1:2 * j + 2, :]
        kpos = ks + lax.broadcasted_iota(jnp.int32, (KEY_TILE, 1), 0)
        valid = ((kpos >> CHUNK_SHIFT) <= qchunk) & (kpos < s_real)
        bits = lax.bitcast_convert_type(acc, jnp.int32)
        key = bits ^ ((bits >> 31) & 0x7FFFFFFF)
        key_ref[pl.ds(ks, KEY_TILE), :] = jnp.where(valid, key, INT_MIN)
        return carry

    lax.fori_loop(0, n_valid, idx_body, 0)

    fold = 64

    def count_ge(cand):
        def body(t, acc):
            c = jnp.where(key_ref[pl.ds(tile_start(t), KEY_TILE), :] >= cand, 1.0, 0.0)
            return acc + jnp.sum(c.reshape(KEY_TILE // fold, fold, tq), axis=0)
        acc = lax.fori_loop(0, n_valid, body, jnp.zeros((fold, tq), F32))
        return jnp.sum(acc, axis=0, keepdims=True)

    thr = _topk_threshold(count_ge, (1, tq), topk)

    def mb_body(t, carry):
        ks = tile_start(t)
        mb_ref[pl.ds(ks, KEY_TILE), :] = jnp.where(key_ref[pl.ds(ks, KEY_TILE), :] >= thr, 0.0, NEG)
        return carry

    lax.fori_loop(0, n_valid, mb_body, 0)

    m_ref[...] = jnp.full(m_ref.shape, NEG, F32)
    l_ref[...] = jnp.zeros(l_ref.shape, F32)
    acc_ref[...] = jnp.zeros(acc_ref.shape, F32)

    def tile(t, near):
        ks = tile_start(t)
        mb = mb_ref[pl.ds(ks, KEY_TILE), :]
        mb = jnp.concatenate([mb] * group, axis=1)

        def scores(g):
            s_t = _dot_nt(k_ref[pl.ds(ks, KEY_TILE), g * dh:(g + 1) * dh], qg_ref[g]) + mb
            return s_t + tab_ref[qb - 2 * t, g] if near else s_t

        s_next = scores(0)
        for g in range(kv_heads):
            s_t = s_next
            if g + 1 < kv_heads:
                s_next = scores(g + 1)
            _online_update_t(s_t, vt_ref[g * dh:(g + 1) * dh, pl.ds(ks, KEY_TILE)], m_ref, l_ref, acc_ref, g)

    def far_body(t, carry):
        tile(t, False)
        return carry

    def near_body(t, carry):
        tile(t, True)
        return carry

    lax.fori_loop(0, n_far, far_body, 0)
    lax.fori_loop(n_far, n_valid, near_body, 0)
    for g in range(kv_heads):
        o_t = acc_ref[g] / l_ref[g]
        for r in range(group):
            h = group * g + r
            o_ref[:, h * dh:(h + 1) * dh] = o_t[:, r * tq:(r + 1) * tq].T.astype(BF)


def _dsa_sample_kernel(q_ref, qi_ref, wi_ref, kdc_ref, kdn_ref, kn_ref, vn_ref, tab_ref, kc_ref, vc_ref, dst_ref,
                       o_ref, key_ref, qim_ref, *, t, past, topk, kv_heads, group, idx_dim):
    del dst_ref
    dh = LANES
    idx_heads = qim_ref.shape[0] // t

    def cached(ref, g, c):
        return ref[pl.ds(c * KEY_TILE * kv_heads + g, KEY_TILE, stride=kv_heads), :].astype(BF)

    qb = past // LANES
    n_c = past // KEY_TILE
    s_real = past + t
    qchunk = (past + lax.broadcasted_iota(jnp.int32, (t, 1), 0)) >> CHUNK_SHIFT

    lane = lax.broadcasted_iota(jnp.int32, (t, LANES), 1)
    per_pair = LANES // idx_dim
    wi = wi_ref[...]
    w_rows = []
    for h in range(idx_heads):
        pair = qi_ref[:, (h // per_pair) * LANES:(h // per_pair + 1) * LANES].astype(F32)
        sub = h % per_pair
        keep = (lane >= sub * idx_dim) & (lane < (sub + 1) * idx_dim)
        qim_ref[h * t:(h + 1) * t, :] = jnp.where(keep, pair, 0.0).astype(BF)
        w_rows.append(jnp.broadcast_to(wi[:, h:h + 1], (t, LANES)))
    w_rows = jnp.concatenate(w_rows, axis=0)

    def keys_of(kd, kpos):
        w = kd.shape[0]
        sc = jnp.maximum(_dot_nt(qim_ref[...], kd), 0.0) * jnp.concatenate([w_rows] * (w // LANES), axis=1)
        acc = jnp.sum(sc.reshape(idx_heads, t, w), axis=0)
        valid = ((kpos >> CHUNK_SHIFT) <= qchunk) & (kpos < s_real)
        bits = lax.bitcast_convert_type(acc, jnp.int32)
        key = bits ^ ((bits >> 31) & 0x7FFFFFFF)
        return jnp.where(valid, key, INT_MIN)

    for c in range(n_c):
        kpos = c * KEY_TILE + lax.broadcasted_iota(jnp.int32, (t, KEY_TILE), 1)
        key_ref[:, c * KEY_TILE:(c + 1) * KEY_TILE] = keys_of(kdc_ref[c * KEY_TILE:(c + 1) * KEY_TILE, :], kpos)
    kpos = past + lax.broadcasted_iota(jnp.int32, (t, LANES), 1)
    key_ref[:, past:] = keys_of(kdn_ref[...].astype(BF), kpos)

    def count_ge(cand):
        return jnp.sum(jnp.where(key_ref[...] >= cand, 1.0, 0.0), axis=-1, keepdims=True)

    thr = _topk_threshold(count_ge, (t, 1), topk)
    mb = jnp.where(key_ref[...] >= thr, 0.0, NEG)

    for g in range(kv_heads):
        hs = slice(group * g, group * (g + 1))
        qg = jnp.concatenate(
            [q_ref[:, (group * g + r) * dh:(group * g + r + 1) * dh] for r in range(group)], axis=0)
        parts = []
        for c in range(n_c):
            s = _dot_nt(qg, cached(kc_ref, g, c)).reshape(group, t, KEY_TILE)
            v = qb - 2 * c
            if v <= 2:
                s = s + tab_ref[v, hs]
            parts.append(s)
        s = _dot_nt(qg, kn_ref[:, g * dh:(g + 1) * dh]).reshape(group, t, LANES)
        parts.append(s + tab_ref[0, hs, :, :LANES])
        s_all = jnp.concatenate(parts, axis=-1) + mb[None]
        m = jnp.max(s_all, axis=-1, keepdims=True)
        p = jnp.exp2(s_all - m)
        l = jnp.sum(p, axis=-1, keepdims=True)
        pb = p.astype(BF).reshape(group * t, past + LANES)
        acc = _dot(pb[:, past:], vn_ref[:, g * dh:(g + 1) * dh])
        for c in range(n_c):
            acc = acc + _dot(pb[:, c * KEY_TILE:(c + 1) * KEY_TILE], cached(vc_ref, g, c))
        out = acc.reshape(group, t, dh) / l
        for r in range(group):
            o_ref[:, (group * g + r) * dh:(group * g + r + 1) * dh] = out[r].astype(BF)


def _dsa_prompt(qkv, small, vt, tab_t, *, batch, seq, heads, kv_heads, idx_heads, idx_dim):
    tq = LANES
    assert seq % KEY_TILE == 0
    nt = seq // tq
    dh = LANES
    group = heads // kv_heads
    qcols, kvcols, qicols = heads * dh, kv_heads * dh, idx_heads * idx_dim
    kern = functools.partial(_dsa_prompt_kernel, tq=tq, topk=min(TOPK_MAX, seq // 4),
                             kv_heads=kv_heads, group=group, idx_dim=idx_dim)
    return pl.pallas_call(
        kern,
        out_shape=jax.ShapeDtypeStruct((qkv.shape[0], qcols), BF),
        grid=(batch, nt),
        in_specs=[
            pl.BlockSpec((tq, qcols), lambda b, i: (b * nt + i, 0)),
            pl.BlockSpec((tq, qicols), lambda b, i: (b * nt + i, (qcols + 2 * kvcols) // qicols)),
            pl.BlockSpec((tq, LANES), lambda b, i: (b * nt + i, 1)),
            pl.BlockSpec((seq, kvcols), lambda b, i: (b, qcols // kvcols)),
            pl.BlockSpec((kvcols, seq), lambda b, i: (0, b)),
            pl.BlockSpec((seq, LANES), lambda b, i: (b, 0)),
            pl.BlockSpec((3, kv_heads, KEY_TILE, group * tq), lambda b, i: (0, 0, 0, 0)),
        ],
        out_specs=pl.BlockSpec((tq, qcols), lambda b, i: (b * nt + i, 0)),
        scratch_shapes=[pltpu.VMEM((seq, tq), jnp.int32),
                        pltpu.VMEM((seq, tq), F32),
                        pltpu.VMEM((idx_heads * idx_dim // LANES, 2 * tq, LANES), BF),
                        pltpu.VMEM((kv_heads, group * tq, dh), BF),
                        pltpu.VMEM((kv_heads, 1, group * tq), F32),
                        pltpu.VMEM((kv_heads, 1, group * tq), F32),
                        pltpu.VMEM((kv_heads, dh, group * tq), F32)],
        compiler_params=_cparams("parallel", "arbitrary"),
        name="dsa_prompt",
    )(qkv, qkv, small, qkv, vt, small, tab_t)


def _dsa_sample(qkv, small, kd_cache, kd_new, k_new, v_new, cache_k, cache_v, tab, dst, *, layer, row0, streams, t,
                heads, kv_heads, idx_heads, idx_dim):
    n_layers, _, past = cache_k.shape[:3]
    dh = LANES
    group = heads // kv_heads
    qcols, kvcols, qicols = heads * dh, kv_heads * dh, idx_heads * idx_dim
    assert past % KEY_TILE == 0 and t <= LANES and row0 % t == 0
    rb0 = row0 // t
    kern = functools.partial(_dsa_sample_kernel, t=t, past=past, topk=min(TOPK_MAX, (past + t) // 4),
                             kv_heads=kv_heads, group=group, idx_dim=idx_dim)
    rows = lambda c: c.reshape(n_layers, streams, past * kv_heads, dh)
    cache_spec = pl.BlockSpec((None, None, past * kv_heads, dh), lambda b: (layer, b, 0, 0))

    return pl.pallas_call(
        kern,
        out_shape=jax.ShapeDtypeStruct(dst.shape, dst.dtype),
        grid=(streams,),
        in_specs=[
            pl.BlockSpec((t, qcols), lambda b: (rb0 + b, 0)),
            pl.BlockSpec((t, qicols), lambda b: (rb0 + b, (qcols + 2 * kvcols) // qicols)),
            pl.BlockSpec((t, LANES), lambda b: (rb0 + b, 1)),
            pl.BlockSpec((past, LANES), lambda b: (b, 0)),
            pl.BlockSpec((LANES, LANES), lambda b: (b, 0)),
            pl.BlockSpec((LANES, kvcols), lambda b: (b, 0)),
            pl.BlockSpec((LANES, kvcols), lambda b: (b, 0)),
            pl.BlockSpec((3, heads, t, KEY_TILE), lambda b: (0, 0, 0, 0)),
            cache_spec,
            cache_spec,
            pl.BlockSpec(memory_space=pl.ANY),
        ],
        out_specs=pl.BlockSpec((t, qcols), lambda b: (rb0 + b, 0)),
        input_output_aliases={10: 0},
        scratch_shapes=[pltpu.VMEM((t, past + LANES), jnp.int32),
                        pltpu.VMEM((idx_heads * t, LANES), BF)],
        compiler_params=_cparams("parallel"),
        name="dsa_sample",
    )(qkv, qkv, small, kd_cache, kd_new, k_new, v_new, tab, rows(cache_k), rows(cache_v), dst)


def _rope_tables(pos, rope):
    half = rope // 2
    inv_freq = ROPE_THETA ** (-jnp.arange(half, dtype=F32) / half)
    ang = pos.astype(F32)[:, None] * inv_freq[None, :]
    cos, sin = jnp.cos(ang), jnp.sin(ang)
    pad = jnp.zeros((pos.shape[0], LANES - rope), F32)
    return (jnp.concatenate([cos, cos, pad], axis=1), jnp.concatenate([sin, sin, pad], axis=1))


def _swap_halves(w):
    half = w.shape[-1] // 2
    return jnp.concatenate([-w[..., half:], w[..., :half]], axis=-1)


@jax.jit
def _step(x_prompt, x_sample, cache_mla_latent, cache_mla_krope, cache_dsa_k, cache_dsa_v, cache_dsa_kidx,
          norm_g, norm_f, ffn_w_in, ffn_w_out, mla_w_down, mla_g_q, mla_g_kv, mla_w_uq, mla_w_uk, mla_w_uv,
          mla_w_o, dsa_w_in, dsa_g_kidx, dsa_w_o, rel_bias):
    batch, seq, d_model = x_prompt.shape
    streams, dec, _ = x_sample.shape
    depth = norm_g.shape[0]
    past = cache_mla_latent.shape[2]
    kv_lora = cache_mla_latent.shape[3]
    rope = cache_mla_krope.shape[3]
    q_lora = mla_w_uq.shape[1]
    mla_heads = mla_w_uq.shape[2]
    nope = mla_w_uk.shape[3]
    kv_heads, dh = cache_dsa_k.shape[3], cache_dsa_k.shape[4]
    idx_dim = cache_dsa_kidx.shape[3]
    dsa_heads = dsa_w_o.shape[1] // dh
    idx_heads = rel_bias.shape[1]
    assert nope == LANES and dh == LANES and 2 * rope == LANES and mla_w_uv.shape[3] == LANES
    assert 2 * idx_dim == LANES and idx_heads <= LANES
    mp, ms = batch * seq, streams * dec
    m = mp + ms
    tm = m // 8
    assert tm * 8 == m and tm % 16 == 0
    qcols, kvcols, qicols = dsa_heads * dh, kv_heads * dh, idx_heads * idx_dim
    big = qcols + 2 * kvcols + qicols

    x = jnp.concatenate([x_prompt.reshape(mp, d_model), x_sample.reshape(ms, d_model)], axis=0)
    g3 = norm_g.reshape(depth * 3, 1, d_model)
    pos = jnp.concatenate([jnp.tile(jnp.arange(seq, dtype=jnp.int32), batch),
                           jnp.tile(past + jnp.arange(dec, dtype=jnp.int32), streams)])
    cos_t, sin_t = _rope_tables(pos, rope)
    tab = _bias_tab(rel_bias, dec)
    tab_t = _bias_tab_t(rel_bias, kv_heads, LANES)

    def ffn(x, li, s):
        act = _ffn_in(x, g3, li * 3 + 2 * s, ffn_w_in, (li, s), tm, 512)
        return _mm_res(act, ffn_w_out, (li, s), x, 0.5, tm, 256)

    def new_slab(a):
        a = a.reshape(streams, dec, a.shape[-1])
        return jnp.pad(a, ((0, 0), (0, LANES - dec), (0, 0))).reshape(streams * LANES, a.shape[-1])

    new_c, new_kr, new_k, new_v, new_ki = [], [], [], [], []
    for li in range(depth):
        j = li // 2
        gi = li * 3 + 1
        x = ffn(x, li, 0)
        if li % 2 == 0:
            w_kr = mla_w_down[j, :, q_lora + kv_lora:]
            w_kr2 = jnp.concatenate([w_kr, _swap_halves(w_kr)], axis=1)[None]
            d = _norm_mm(x, g3, gi, mla_w_down, (j,), (q_lora + kv_lora) // 1024, 1024, F32, tm)
            kr2 = _norm_mm(x, g3, gi, w_kr2, (0,), 1, LANES, F32, tm)
            cq, ckv, aug, kr = _mla_post(d, kr2, mla_g_q[j], mla_g_kv[j], cos_t, sin_t, tm // 2,
                                         q_lora, kv_lora, rope)
            new_c.append(ckv)
            new_kr.append(kr)
            uq = mla_w_uq[j]
            uq_rope = uq[..., nope:]
            wq = jnp.concatenate([uq[..., :nope], uq_rope, _swap_halves(uq_rope)], axis=-1)
            wq = jnp.transpose(wq, (1, 0, 2)).astype(BF)
            scale = (nope + rope) ** -0.5 * LOG2E
            q_arr = _mla_qup(cq, wq, cos_t, sin_t, tm, rope, scale, 4)
            w_kexp = jnp.zeros((kv_lora + LANES, mla_heads, 2 * LANES), F32)
            w_kexp = w_kexp.at[:kv_lora, :, :nope].set(mla_w_uk[j])
            w_kexp = w_kexp.at[kv_lora:kv_lora + rope, :, nope:nope + rope].set(
                jnp.broadcast_to(jnp.eye(rope, dtype=F32)[:, None, :], (rope, mla_heads, rope)))
            w_kexp = w_kexp.reshape(kv_lora + LANES, mla_heads * 2 * LANES).astype(BF)
            k_arr = _mm(aug, w_kexp, mp, kv_lora + LANES, 1024, 2048, BF)
            wuv_rows = jnp.transpose(mla_w_uv[j], (1, 2, 0)).reshape(mla_heads * LANES, kv_lora).astype(BF)
            vt_arr = _mm_nt(wuv_rows, aug, mp, kv_lora, 512, 1024, BF)
            o = _mla_attn_prompt(q_arr, k_arr, vt_arr, batch, seq, mla_heads, 512, 4, m)
            wuk_t = jnp.transpose(mla_w_uk[j], (1, 0, 2)).astype(BF)
            wuv_t = jnp.transpose(mla_w_uv[j], (1, 0, 2)).astype(BF)
            q_lat, q_rope = _mla_qlat(q_arr, wuk_t, mp, ms)
            kr_pad = jnp.pad(cache_mla_krope.astype(BF), ((0, 0), (0, 0), (0, 0), (0, LANES - rope)))
            o_lat = _mla_attn_sample(q_lat, q_rope, cache_mla_latent, kr_pad, aug, j, mp, streams, dec)
            o = _mla_oup(o_lat, wuv_t, o, mp)
            x = _mm_res(o, mla_w_o, (j,), x, 1.0, tm, 512)
        else:
            assert 2 * kvcols == 1024 and qcols % 1024 == 0
            qkv, kv32 = _norm_mm(x, g3, gi, dsa_w_in, (j,), big // 1024, 1024, BF, tm,
                                 scale_blocks=(qcols // 1024, dh ** -0.5 * LOG2E), keep_blk=qcols // 1024)
            w_ki = dsa_w_in[j, :, big:big + idx_dim]
            w_wi = dsa_w_in[j, :, big + idx_dim:]
            w_small = jnp.concatenate([w_ki, w_ki, w_wi, jnp.zeros((d_model, LANES - idx_heads), F32)], axis=1)
            gk2 = jnp.tile(dsa_g_kidx[j], 2).reshape(1, LANES)
            small = _dsa_small(x, g3, gi, w_small, gk2, tm, idx_dim, (idx_dim ** -0.5) * (idx_heads ** -0.5))
            new_k.append(kv32[:, :kvcols])
            new_v.append(kv32[:, kvcols:])
            new_ki.append(small[:, :idx_dim])
            common = dict(heads=dsa_heads, kv_heads=kv_heads, idx_heads=idx_heads, idx_dim=idx_dim)
            vt = qkv[:mp, qcols + kvcols:qcols + 2 * kvcols].T
            o = _dsa_prompt(qkv, small, vt, tab_t, batch=batch, seq=seq, **common)
            ki_c = cache_dsa_kidx[j].astype(BF).reshape(streams * past, idx_dim)
            o = _dsa_sample(qkv, small, jnp.concatenate([ki_c, ki_c], axis=-1), new_slab(small[mp:, :LANES]),
                            new_slab(qkv[mp:, qcols:qcols + kvcols]),
                            new_slab(qkv[mp:, qcols + kvcols:qcols + 2 * kvcols]),
                            cache_dsa_k, cache_dsa_v, tab, o, layer=j, row0=mp, streams=streams, t=dec, **common)
            x = _mm_res(o, dsa_w_o, (j,), x, 1.0, tm, 512)
        x = ffn(x, li, 1)

    tf = math.gcd(mp, ms)
    y_p = _final_norm(x, norm_f, 0, mp, tf)
    y_s = _final_norm(x, norm_f, mp, ms, tf)

    def split(rows, *tail):
        a = jnp.stack(rows)
        n = a.shape[0]
        return (a[:, :mp].reshape(n, batch, seq, *tail), a[:, mp:].reshape(n, streams, dec, *tail))

    p_c, s_c = split(new_c, kv_lora)
    p_kr, s_kr = split(new_kr, rope)
    p_k, s_k = split(new_k, kv_heads, dh)
    p_v, s_v = split(new_v, kv_heads, dh)
    p_ki, s_ki = split(new_ki, idx_dim)
    return (y_p.reshape(batch, seq, d_model), y_s.reshape(streams, dec, d_model),
            p_c, p_kr, p_k, p_v, p_ki, s_c, s_kr, s_k, s_v, s_ki)


def kernel(x_prompt, x_sample, cache_mla_latent, cache_mla_krope, cache_dsa_k, cache_dsa_v, cache_dsa_kidx,
           norm_g, norm_f, ffn_w_in, ffn_w_out, mla_w_down, mla_g_q, mla_g_kv, mla_w_uq, mla_w_uk, mla_w_uv,
           mla_w_o, dsa_w_in, dsa_g_kidx, dsa_w_o, rel_bias):
    return _step(x_prompt, x_sample, cache_mla_latent, cache_mla_krope, cache_dsa_k, cache_dsa_v,
                 cache_dsa_kidx, norm_g, norm_f, ffn_w_in, ffn_w_out, mla_w_down, mla_g_q, mla_g_kv,
                 mla_w_uq, mla_w_uk, mla_w_uv, mla_w_o, dsa_w_in, dsa_g_kidx, dsa_w_o, rel_bias)
```

```python
import functools
import math

import jax
import jax.numpy as jnp
from jax import lax
from jax.experimental import pallas as pl
from jax.experimental.pallas import tpu as pltpu

BF = jnp.bfloat16
F32 = jnp.float32

CHUNK = 64
ROPE_THETA = 10000.0
NORM_EPS = 1e-6
TOPK_MAX = 256
REL_BUCKETS = 32
REL_MAX_DIST = 128

LANES = 128
KEY_TILE = 2 * LANES
VMEM_LIMIT = 56 * 1024 * 1024
NEG = -1e30
INT_MIN = -(2 ** 31)
LOG2E = math.log2(math.e)
CHUNK_SHIFT = CHUNK.bit_length() - 1
assert 1 << CHUNK_SHIFT == CHUNK

NT = (((1,), (1,)), ((), ()))


def _cparams(*sem):
    return pltpu.CompilerParams(dimension_semantics=sem, vmem_limit_bytes=VMEM_LIMIT)


def _rms(x, g):
    ms = jnp.mean(x * x, axis=-1, keepdims=True)
    return x * lax.rsqrt(ms + NORM_EPS) * g


def _dot(a, b):
    return jnp.dot(a, b, preferred_element_type=F32)


def _dot_nt(a, b):
    return lax.dot_general(a, b, NT, preferred_element_type=F32)


def _wspec(prefix, k, tn, col0=0):
    return pl.BlockSpec((None,) * len(prefix) + (k, tn),
                        lambda i, j: tuple(prefix) + (0, col0 + j))


def _ffn_in_kernel(x_ref, g_ref, wg_ref, wu_ref, o_ref, h_ref):
    @pl.when(pl.program_id(1) == 0)
    def _():
        h_ref[...] = _rms(x_ref[...], g_ref[...]).astype(BF)

    kc = 512
    gate = up = None
    for k0 in range(0, h_ref.shape[1], kc):
        h = h_ref[:, k0:k0 + kc]
        g = _dot(h, wg_ref[k0:k0 + kc, :].astype(BF))
        u = _dot(h, wu_ref[k0:k0 + kc, :].astype(BF))
        gate = g if gate is None else gate + g
        up = u if up is None else up + u
    o_ref[...] = (gate / (1.0 + jnp.exp(-gate)) * up).astype(BF)


def _ffn_in(x, g3, gi, w_in, prefix, tm, tf):
    m, d = x.shape
    d_ff = w_in.shape[-1] // 2
    nf = d_ff // tf
    return pl.pallas_call(
        _ffn_in_kernel,
        out_shape=jax.ShapeDtypeStruct((m, d_ff), BF),
        grid=(m // tm, nf),
        in_specs=[
            pl.BlockSpec((tm, d), lambda i, j: (i, 0)),
            pl.BlockSpec((None, 1, d), lambda i, j: (gi, 0, 0)),
            _wspec(prefix, d, tf),
            _wspec(prefix, d, tf, nf),
        ],
        out_specs=pl.BlockSpec((tm, tf), lambda i, j: (i, j)),
        scratch_shapes=[pltpu.VMEM((tm, d), BF)],
        compiler_params=_cparams("parallel", "arbitrary"),
        name="ffn_in",
    )(x, g3, w_in, w_in)


def _mm_res_kernel(a_ref, w_ref, x_ref, o_ref, *, scale):
    acc = _dot(a_ref[...], w_ref[...].astype(BF))
    o_ref[...] = x_ref[...] + scale * acc


def _mm_res(a, w, prefix, x, scale, tm, tn):
    m, k = a.shape
    n = x.shape[1]
    return pl.pallas_call(
        functools.partial(_mm_res_kernel, scale=scale),
        out_shape=jax.ShapeDtypeStruct((m, n), F32),
        grid=(m // tm, n // tn),
        in_specs=[
            pl.BlockSpec((tm, k), lambda i, j: (i, 0)),
            _wspec(prefix, k, tn),
            pl.BlockSpec((tm, tn), lambda i, j: (i, j)),
        ],
        out_specs=pl.BlockSpec((tm, tn), lambda i, j: (i, j)),
        compiler_params=_cparams("parallel", "arbitrary"),
        name="mm_res",
    )(a, w, x)


def _norm_mm_kernel(x_ref, g_ref, w_ref, o_ref, *rest, scale_blocks, keep_blk):
    h_ref = rest[-1]
    j = pl.program_id(1)

    @pl.when(j == 0)
    def _():
        h_ref[...] = _rms(x_ref[...], g_ref[...]).astype(BF)

    acc = _dot_nt(h_ref[...], w_ref[...].astype(BF))
    if keep_blk is not None:
        @pl.when(j == keep_blk)
        def _():
            rest[0][...] = acc
    if scale_blocks is not None:
        nb, val = scale_blocks
        acc = acc * jnp.where(j < nb, val, 1.0).astype(F32)
    o_ref[...] = acc.astype(o_ref.dtype)


def _norm_mm(x, g3, gi, w_t, prefix, row0_blk, n_blk, tn, out_dtype, tm, scale_blocks=None, keep_blk=None):
    m, d = x.shape
    out_shape = [jax.ShapeDtypeStruct((m, n_blk * tn), out_dtype)]
    out_specs = [pl.BlockSpec((tm, tn), lambda i, j: (i, j))]
    if keep_blk is not None:
        out_shape.append(jax.ShapeDtypeStruct((m, tn), F32))
        out_specs.append(pl.BlockSpec((tm, tn), lambda i, j: (i, 0)))
    out = pl.pallas_call(
        functools.partial(_norm_mm_kernel, scale_blocks=scale_blocks, keep_blk=keep_blk),
        out_shape=out_shape,
        grid=(m // tm, n_blk),
        in_specs=[
            pl.BlockSpec((tm, d), lambda i, j: (i, 0), pipeline_mode=pl.Buffered(1)),
            pl.BlockSpec((None, 1, d), lambda i, j: (gi, 0, 0)),
            pl.BlockSpec((None,) * len(prefix) + (tn, d), lambda i, j: tuple(prefix) + (row0_blk + j, 0)),
        ],
        out_specs=out_specs,
        scratch_shapes=[pltpu.VMEM((tm, d), BF)],
        compiler_params=_cparams("parallel", "arbitrary"),
        name="norm_mm",
    )(x, g3, w_t)
    return out if keep_blk is not None else out[0]


def _mm_kernel(a_ref, w_ref, o_ref):
    o_ref[...] = _dot(a_ref[...], w_ref[...].astype(BF)).astype(o_ref.dtype)


def _mm(a, w, m_rows, k_cols, tm, tn, out_dtype):
    n = w.shape[1]
    return pl.pallas_call(
        _mm_kernel,
        out_shape=jax.ShapeDtypeStruct((m_rows, n), out_dtype),
        grid=(m_rows // tm, n // tn),
        in_specs=[
            pl.BlockSpec((tm, k_cols), lambda i, j: (i, 0)),
            pl.BlockSpec((k_cols, tn), lambda i, j: (0, j)),
        ],
        out_specs=pl.BlockSpec((tm, tn), lambda i, j: (i, j)),
        compiler_params=_cparams("parallel", "arbitrary"),
        name="mm",
    )(a, w)


def _mm_nt_kernel(w_ref, a_ref, o_ref):
    o_ref[...] = _dot_nt(w_ref[...], a_ref[...]).astype(o_ref.dtype)


def _mm_nt(w_t, a, m_rows, k_cols, tn_rows, tm_cols, out_dtype):
    n = w_t.shape[0]
    return pl.pallas_call(
        _mm_nt_kernel,
        out_shape=jax.ShapeDtypeStruct((n, m_rows), out_dtype),
        grid=(n // tn_rows, m_rows // tm_cols),
        in_specs=[
            pl.BlockSpec((tn_rows, k_cols), lambda i, j: (i, 0)),
            pl.BlockSpec((tm_cols, k_cols), lambda i, j: (j, 0)),
        ],
        out_specs=pl.BlockSpec((tn_rows, tm_cols), lambda i, j: (i, j)),
        compiler_params=_cparams("parallel", "arbitrary"),
        name="mm_nt",
    )(w_t, a)


def _final_norm_kernel(x_ref, g_ref, o_ref):
    o_ref[...] = _rms(x_ref[...], g_ref[...])


def _final_norm(x, g, row0, rows, tm):
    d = x.shape[1]
    assert row0 % tm == 0 and rows % tm == 0
    return pl.pallas_call(
        _final_norm_kernel,
        out_shape=jax.ShapeDtypeStruct((rows, d), F32),
        grid=(rows // tm,),
        in_specs=[pl.BlockSpec((tm, d), lambda i: (row0 // tm + i, 0)),
                  pl.BlockSpec((1, d), lambda i: (0, 0))],
        out_specs=pl.BlockSpec((tm, d), lambda i: (i, 0)),
        compiler_params=_cparams("parallel"),
        name="final_norm",
    )(x, g.reshape(1, d))


def _rope_rotate(x, cos_t, sin_t, rope):
    half = rope // 2
    lane = lax.broadcasted_iota(jnp.int32, x.shape, 1)
    rot_half = jnp.where(lane < half, -pltpu.roll(x, LANES - half, 1), pltpu.roll(x, half, 1))
    return x * cos_t + rot_half * sin_t


def _mla_post_kernel(d_ref, kr2_ref, gq_ref, gkv_ref, c_ref, s_ref,
                     cq_ref, ckv_ref, aug_ref, kr_ref, *, q_lora, kv_lora, rope):
    d = d_ref[...]
    cq_ref[...] = _rms(d[:, :q_lora], gq_ref[...]).astype(BF)
    ckv = _rms(d[:, q_lora:q_lora + kv_lora], gkv_ref[...])
    ckv_ref[...] = ckv
    kr = _rope_rotate(kr2_ref[...], c_ref[...], s_ref[...], rope)
    aug_ref[:, :kv_lora] = ckv.astype(BF)
    aug_ref[:, kv_lora:] = kr.astype(BF)
    kr_ref[...] = kr[:, :rope]


def _mla_post(d, kr2, g_q, g_kv, cos_t, sin_t, tm, q_lora, kv_lora, rope):
    m = d.shape[0]
    row = lambda w: pl.BlockSpec((tm, w), lambda i: (i, 0))
    vec = lambda w: pl.BlockSpec((1, w), lambda i: (0, 0))
    return pl.pallas_call(
        functools.partial(_mla_post_kernel, q_lora=q_lora, kv_lora=kv_lora, rope=rope),
        out_shape=(jax.ShapeDtypeStruct((m, q_lora), BF),
                   jax.ShapeDtypeStruct((m, kv_lora), F32),
                   jax.ShapeDtypeStruct((m, kv_lora + LANES), BF),
                   jax.ShapeDtypeStruct((m, rope), F32)),
        grid=(m // tm,),
        in_specs=[row(d.shape[1]), row(LANES), vec(q_lora), vec(kv_lora), row(LANES), row(LANES)],
        out_specs=(row(q_lora), row(kv_lora), row(kv_lora + LANES), row(rope)),
        compiler_params=_cparams("parallel"),
        name="mla_post",
    )(d, kr2, g_q.reshape(1, -1), g_kv.reshape(1, -1), cos_t, sin_t)


def _mla_qup_kernel(cq_ref, w_ref, c_ref, s_ref, o_ref, *, rope, scale, hq):
    cq = cq_ref[...]
    k = cq.shape[1]
    for h in range(hq):
        w = jnp.concatenate([w_ref[h], jnp.zeros((2 * LANES - w_ref.shape[1], k), F32)], axis=0)
        acc = _dot_nt(cq, w.astype(BF))
        lo = acc[:, :LANES]
        rot = _rope_rotate(acc[:, LANES:], c_ref[...], s_ref[...], rope)
        o_ref[:, 2 * h * LANES:(2 * h + 1) * LANES] = (lo * scale).astype(BF)
        o_ref[:, (2 * h + 1) * LANES:(2 * h + 2) * LANES] = (rot * scale).astype(BF)


def _mla_qup(cq, wq_t, layer, cos_t, sin_t, tm, rope, scale, hq):
    m, k = cq.shape
    heads, hd = wq_t.shape[1], wq_t.shape[2]
    return pl.pallas_call(
        functools.partial(_mla_qup_kernel, rope=rope, scale=scale, hq=hq),
        out_shape=jax.ShapeDtypeStruct((m, heads * 2 * LANES), BF),
        grid=(m // tm, heads // hq),
        in_specs=[
            pl.BlockSpec((tm, k), lambda i, h: (i, 0)),
            pl.BlockSpec((None, hq, hd, k), lambda i, h: (layer, h, 0, 0)),
            pl.BlockSpec((tm, LANES), lambda i, h: (i, 0)),
            pl.BlockSpec((tm, LANES), lambda i, h: (i, 0)),
        ],
        out_specs=pl.BlockSpec((tm, hq * 2 * LANES), lambda i, h: (i, h)),
        compiler_params=_cparams("parallel", "arbitrary"),
        name="mla_qup",
    )(cq, wq_t, cos_t, sin_t)


def _online_update_t(s_t, vt_tile, m_ref, l_ref, acc_ref, hs):
    m_old = m_ref[hs]
    m_new = jnp.maximum(m_old, jnp.max(s_t, axis=0, keepdims=True))
    alpha = jnp.exp2(m_old - m_new)
    p = jnp.exp2(s_t - m_new)
    if l_ref is not None:
        l_ref[hs] = alpha * l_ref[hs] + jnp.sum(p, axis=0, keepdims=True)
    acc_ref[hs] = alpha * acc_ref[hs] + _dot(vt_tile, p.astype(BF))
    m_ref[hs] = m_new


def _mla_attn_prompt_kernel(q_ref, k_ref, vt_ref, dst_ref, o_ref, m_ref, l_ref, acc_ref, *, tq, hp):
    del dst_ref
    i = pl.program_id(2)
    dk, dv = 2 * LANES, LANES
    m_ref[...] = jnp.full(m_ref.shape, NEG, F32)
    l_ref[...] = jnp.zeros(l_ref.shape, F32)
    acc_ref[...] = jnp.zeros(acc_ref.shape, F32)

    def tile(t, diagonal):
        ks = pl.multiple_of(t * tq, tq)

        def scores(h):
            s_t = _dot_nt(k_ref[pl.ds(ks, tq), h * dk:(h + 1) * dk], q_ref[:, h * dk:(h + 1) * dk])
            if diagonal:
                kc = lax.broadcasted_iota(jnp.int32, (tq, 1), 0) >> CHUNK_SHIFT
                qc = lax.broadcasted_iota(jnp.int32, (1, tq), 1) >> CHUNK_SHIFT
                s_t = jnp.where(kc <= qc, s_t, NEG)
            return s_t

        s_next = scores(0)
        for h in range(hp):
            s_t = s_next
            if h + 1 < hp:
                s_next = scores(h + 1)
            _online_update_t(s_t, vt_ref[h * dv:(h + 1) * dv, pl.ds(ks, tq)], m_ref, l_ref, acc_ref, h)

    def body(t, carry):
        tile(t, False)
        return carry

    lax.fori_loop(0, i, body, 0)
    tile(i, True)
    for h in range(hp):
        o_ref[:, h * dv:(h + 1) * dv] = (acc_ref[h] / l_ref[h]).T.astype(BF)


def _mla_attn_prompt(q_arr, k_arr, vt_arr, dst, batch, seq, heads, tq, hp):
    assert tq % CHUNK == 0 and seq % tq == 0 and heads % hp == 0
    nt = seq // tq
    dk, dv = 2 * LANES, LANES
    return pl.pallas_call(
        functools.partial(_mla_attn_prompt_kernel, tq=tq, hp=hp),
        out_shape=jax.ShapeDtypeStruct(dst.shape, dst.dtype),
        grid=(batch, heads // hp, nt),
        in_specs=[
            pl.BlockSpec((tq, hp * dk), lambda b, h, i: (b * nt + i, h)),
            pl.BlockSpec((seq, hp * dk), lambda b, h, i: (b, h)),
            pl.BlockSpec((hp * dv, seq), lambda b, h, i: (h, b)),
            pl.BlockSpec(memory_space=pl.ANY),
        ],
        input_output_aliases={3: 0},
        out_specs=pl.BlockSpec((tq, hp * dv), lambda b, h, i: (b * nt + i, h)),
        scratch_shapes=[pltpu.VMEM((hp, 1, tq), F32), pltpu.VMEM((hp, 1, tq), F32),
                        pltpu.VMEM((hp, dv, tq), F32)],
        compiler_params=_cparams("parallel", "parallel", "arbitrary"),
        name="mla_attn_prompt",
    )(q_arr, k_arr, vt_arr, dst)


def _mla_qlat_kernel(qn_ref, qr_ref, wuk_ref, ql_ref, qro_ref):
    ql_ref[...] = _dot_nt(qn_ref[...], wuk_ref[...]).astype(BF)
    qro_ref[...] = qr_ref[...]


def _mla_qlat(q_arr, wuk_t, row0, rows):
    heads, kv_lora, _ = wuk_t.shape
    rb = row0 // rows
    return pl.pallas_call(
        _mla_qlat_kernel,
        out_shape=(jax.ShapeDtypeStruct((heads, rows, kv_lora), BF),
                   jax.ShapeDtypeStruct((heads, rows, LANES), BF)),
        grid=(heads,),
        in_specs=[
            pl.BlockSpec((rows, LANES), lambda h: (rb, 2 * h)),
            pl.BlockSpec((rows, LANES), lambda h: (rb, 2 * h + 1)),
            pl.BlockSpec((None, kv_lora, LANES), lambda h: (h, 0, 0)),
        ],
        out_specs=(pl.BlockSpec((None, rows, kv_lora), lambda h: (h, 0, 0)),
                   pl.BlockSpec((None, rows, LANES), lambda h: (h, 0, 0))),
        compiler_params=_cparams("parallel"),
        name="mla_qlat",
    )(q_arr, q_arr, wuk_t)


def _mla_attn_sample_kernel(ql_ref, qr_ref, c_ref, kr_ref, aug_ref, o_ref, *, heads, t, kv_lora, per_blk):
    b = pl.program_id(0)
    rows = heads * t
    ql = ql_ref[...].reshape(rows, kv_lora)
    qr = qr_ref[...].reshape(rows, LANES)
    c = c_ref[...].astype(BF)
    s_c = _dot_nt(ql, c) + _dot_nt(qr, kr_ref[...])
    cn = aug_ref[:, :kv_lora]
    s_n = _dot_nt(ql, cn) + _dot_nt(qr, aug_ref[:, kv_lora:])
    owner = lax.broadcasted_iota(jnp.int32, (1, LANES), 1) // t
    s_n = jnp.where(owner == b % per_blk, s_n, NEG)
    m = jnp.maximum(jnp.max(s_c, axis=-1, keepdims=True), jnp.max(s_n, axis=-1, keepdims=True))
    p_c = jnp.exp2(s_c - m)
    p_n = jnp.exp2(s_n - m)
    l = jnp.sum(p_c, axis=-1, keepdims=True) + jnp.sum(p_n, axis=-1, keepdims=True)
    o = (_dot(p_c.astype(BF), c) + _dot(p_n.astype(BF), cn)) / l
    o_ref[...] = o.reshape(heads, t, kv_lora).astype(BF)


def _mla_attn_sample(q_lat, q_rope, cache_c, cache_kr_pad, aug, layer, row0, streams, t):
    heads, _, kv_lora = q_lat.shape
    past = cache_c.shape[2]
    assert past % CHUNK == 0 and t <= CHUNK and LANES % t == 0 and row0 % LANES == 0
    per_blk = LANES // t
    return pl.pallas_call(
        functools.partial(_mla_attn_sample_kernel, heads=heads, t=t, kv_lora=kv_lora, per_blk=per_blk),
        out_shape=jax.ShapeDtypeStruct((heads, streams * t, kv_lora), BF),
        grid=(streams,),
        in_specs=[
            pl.BlockSpec((heads, t, kv_lora), lambda b: (0, b, 0)),
            pl.BlockSpec((heads, t, LANES), lambda b: (0, b, 0)),
            pl.BlockSpec((None, None, past, kv_lora), lambda b: (layer, b, 0, 0)),
            pl.BlockSpec((None, None, past, LANES), lambda b: (layer, b, 0, 0)),
            pl.BlockSpec((LANES, kv_lora + LANES), lambda b: (row0 // LANES + b // per_blk, 0)),
        ],
        out_specs=pl.BlockSpec((heads, t, kv_lora), lambda b: (0, b, 0)),
        compiler_params=_cparams("parallel"),
        name="mla_attn_sample",
    )(q_lat, q_rope, cache_c, cache_kr_pad, aug)


def _mla_oup_kernel(ol_ref, w_ref, dst_ref, o_ref):
    del dst_ref
    o_ref[...] = _dot(ol_ref[...], w_ref[...]).astype(BF)


def _mla_oup(o_lat, wuv_t, dst, row0):
    heads, rows, kv_lora = o_lat.shape
    dv = wuv_t.shape[2]
    assert row0 % rows == 0
    return pl.pallas_call(
        _mla_oup_kernel,
        out_shape=jax.ShapeDtypeStruct(dst.shape, dst.dtype),
        grid=(heads,),
        in_specs=[pl.BlockSpec((None, rows, kv_lora), lambda h: (h, 0, 0)),
                  pl.BlockSpec((None, kv_lora, dv), lambda h: (h, 0, 0)),
                  pl.BlockSpec(memory_space=pl.ANY)],
        out_specs=pl.BlockSpec((rows, dv), lambda h: (row0 // rows, h)),
        input_output_aliases={2: 0},
        compiler_params=_cparams("parallel"),
        name="mla_oup",
    )(o_lat, wuv_t, dst)


def _dsa_small_kernel(x_ref, g_ref, w_ref, gk_ref, o_ref, *, idx_dim, wi_scale):
    h = _rms(x_ref[...], g_ref[...]).astype(BF)
    y = _dot_nt(h, w_ref[...].astype(BF))
    lo = y[:, :LANES]
    lane = lax.broadcasted_iota(jnp.int32, lo.shape, 1)
    ms = jnp.sum(jnp.where(lane < idx_dim, lo * lo, 0.0), axis=-1, keepdims=True) / idx_dim
    o_ref[:, :LANES] = lo * lax.rsqrt(ms + NORM_EPS) * gk_ref[...]
    o_ref[:, LANES:] = y[:, LANES:] * wi_scale


def _dsa_small(x, g3, gi, w_small, gk2, tm, idx_dim, wi_scale):
    m, d = x.shape
    return pl.pallas_call(
        functools.partial(_dsa_small_kernel, idx_dim=idx_dim, wi_scale=wi_scale),
        out_shape=jax.ShapeDtypeStruct((m, 2 * LANES), F32),
        grid=(m // tm,),
        in_specs=[pl.BlockSpec((tm, d), lambda i: (i, 0)),
                  pl.BlockSpec((None, 1, d), lambda i: (gi, 0, 0)),
                  pl.BlockSpec((2 * LANES, d), lambda i: (0, 0)),
                  pl.BlockSpec((1, LANES), lambda i: (0, 0))],
        out_specs=pl.BlockSpec((tm, 2 * LANES), lambda i: (i, 0)),
        compiler_params=_cparams("parallel"),
        name="dsa_small",
    )(x, g3, w_small, gk2)


def _bias_tab_kernel(rb_ref, o_ref, *, keys_on_lanes):
    h = pl.program_id(0)
    half = REL_BUCKETS // 2
    max_exact = half // 2
    if keys_on_lanes:
        shape = (o_ref.shape[1], LANES)
        q_axis, k_axis = 0, 1
    else:
        shape = (LANES, o_ref.shape[2])
        q_axis, k_axis = 1, 0
    qi = lax.broadcasted_iota(jnp.int32, shape, q_axis)
    kj = lax.broadcasted_iota(jnp.int32, shape, k_axis)
    far = rb_ref[half - 1, h]
    for v in range(o_ref.shape[0]):
        for side_idx, off in enumerate((v, v - 1)):
            dist = off * LANES + qi - kj
            side = jnp.where(dist < 0, half, 0)
            a = jnp.abs(dist)
            a_f = jnp.maximum(a, 1).astype(F32)
            large = max_exact + (jnp.log(a_f / max_exact) / math.log(REL_MAX_DIST / max_exact)
                                 * (half - max_exact)).astype(jnp.int32)
            large = jnp.minimum(large, half - 1)
            bucket = side + jnp.where(a < max_exact, a, large)
            bias = jnp.zeros(shape, F32)
            for bk in range(REL_BUCKETS):
                bias = jnp.where(bucket == bk, rb_ref[bk, h], bias)
            bias = (bias - far) * LOG2E
            if keys_on_lanes:
                o_ref[v, :, side_idx * LANES:(side_idx + 1) * LANES] = bias
            else:
                o_ref[v, side_idx * LANES:(side_idx + 1) * LANES, :] = bias


def _bias_tab(rel_bias, rows):
    heads = rel_bias.shape[1]
    return pl.pallas_call(
        functools.partial(_bias_tab_kernel, keys_on_lanes=True),
        out_shape=jax.ShapeDtypeStruct((3, heads, rows, KEY_TILE), F32),
        grid=(heads,),
        in_specs=[pl.BlockSpec(memory_space=pltpu.SMEM)],
        out_specs=pl.BlockSpec((3, None, rows, KEY_TILE), lambda h: (0, h, 0, 0)),
        compiler_params=_cparams("parallel"),
        name="bias_tab",
    )(rel_bias)


def _bias_tab_t(rel_bias, kv_heads, tq):
    heads = rel_bias.shape[1]
    group = heads // kv_heads
    return pl.pallas_call(
        functools.partial(_bias_tab_kernel, keys_on_lanes=False),
        out_shape=jax.ShapeDtypeStruct((3, kv_heads, KEY_TILE, group * tq), F32),
        grid=(heads,),
        in_specs=[pl.BlockSpec(memory_space=pltpu.SMEM)],
        out_specs=pl.BlockSpec((3, None, KEY_TILE, tq), lambda h: (0, h // group, 0, h % group)),
        compiler_params=_cparams("parallel"),
        name="bias_tab_t",
    )(rel_bias)


def _topk_threshold(count_ge, shape, topk):
    thr = jnp.where(count_ge(jnp.zeros(shape, jnp.int32)) >= topk, 0, INT_MIN).astype(jnp.int32)

    def bit_body(b, thr):
        cand = thr | (jnp.int32(1) << (30 - b))
        return jnp.where(count_ge(cand) >= topk, cand, thr)

    thr = lax.fori_loop(0, 31, bit_body, thr)
    return jnp.maximum(thr, INT_MIN + 1)


def _dsa_prompt_kernel(q_ref, qi_ref, wi_ref, k_ref, vt_ref, kd_ref, tab_ref, dst_ref, o_ref,
                       key_ref, mb_ref, qim_ref, qg_ref, m_ref, acc_ref, *, tq, topk, kv_heads, group, idx_dim):
    i = pl.program_id(1)
    qb = i * (tq // LANES)
    n_valid = qb // 2 + 1
    n_far = jnp.maximum((qb + 1) // 2 - 1, 0)
    qchunk = (i * tq + lax.broadcasted_iota(jnp.int32, (1, tq), 1)) >> CHUNK_SHIFT
    s_real = k_ref.shape[0]
    dh = LANES
    n_pairs = qim_ref.shape[0]

    lane = lax.broadcasted_iota(jnp.int32, (tq, LANES), 1)
    for j in range(n_pairs):
        pair = qi_ref[:, j * LANES:(j + 1) * LANES].astype(F32)
        qim_ref[j, :tq] = jnp.where(lane < idx_dim, pair, 0.0).astype(BF)
        qim_ref[j, tq:] = jnp.where(lane >= idx_dim, pair, 0.0).astype(BF)
    for g in range(kv_heads):
        qg_ref[g] = jnp.concatenate(
            [q_ref[:, (group * g + r) * dh:(group * g + r + 1) * dh] for r in range(group)], axis=0)
    wi_t = wi_ref[...].T

    def tile_start(t):
        return pl.multiple_of(t * KEY_TILE, KEY_TILE)

    def idx_body(t, carry):
        ks = tile_start(t)
        kd = kd_ref[pl.ds(ks, KEY_TILE), :].astype(BF)
        acc = jnp.zeros((KEY_TILE, tq), F32)
        for j in range(n_pairs):
            sc = _dot_nt(kd, qim_ref[j])
            acc = acc + jnp.maximum(sc[:, :tq], 0.0) * wi_t[2 * j:2 * j + 1, :]
            acc = acc + jnp.maximum(sc[:, tq:], 0.0) * wi_t[2 * j + 1:2 * j + 2, :]
        kpos = ks + lax.broadcasted_iota(jnp.int32, (KEY_TILE, 1), 0)
        valid = ((kpos >> CHUNK_SHIFT) <= qchunk) & (kpos < s_real)
        bits = lax.bitcast_convert_type(acc, jnp.int32)
        key = bits ^ ((bits >> 31) & 0x7FFFFFFF)
        key_ref[pl.ds(ks, KEY_TILE), :] = jnp.where(valid, key, INT_MIN)
        return carry

    lax.fori_loop(0, n_valid, idx_body, 0)

    fold = 64

    def count_ge(cand):
        def body(t, acc):
            c = jnp.where(key_ref[pl.ds(tile_start(t), KEY_TILE), :] >= cand, 1.0, 0.0)
            return acc + jnp.sum(c.reshape(KEY_TILE // fold, fold, tq), axis=0)
        acc = lax.fori_loop(0, n_valid, body, jnp.zeros((fold, tq), F32))
        return jnp.sum(acc, axis=0, keepdims=True)

    thr = _topk_threshold(count_ge, (1, tq), topk)

    def mb_body(t, carry):
        ks = tile_start(t)
        mb_ref[pl.ds(ks, KEY_TILE), :] = jnp.where(key_ref[pl.ds(ks, KEY_TILE), :] >= thr, 0.0, NEG)
        return carry

    lax.fori_loop(0, n_valid, mb_body, 0)

    m_ref[...] = jnp.full(m_ref.shape, NEG, F32)
    acc_ref[...] = jnp.zeros(acc_ref.shape, F32)
    dva = acc_ref.shape[1]

    def tile(t, near):
        ks = tile_start(t)
        mb = mb_ref[pl.ds(ks, KEY_TILE), :]
        mb = jnp.concatenate([mb] * group, axis=1)

        def scores(g):
            s_t = _dot_nt(k_ref[pl.ds(ks, KEY_TILE), g * dh:(g + 1) * dh], qg_ref[g]) + mb
            return s_t + tab_ref[qb - 2 * t, g] if near else s_t

        s_next = scores(0)
        for g in range(kv_heads):
            s_t = s_next
            if g + 1 < kv_heads:
                s_next = scores(g + 1)
            _online_update_t(s_t, vt_ref[g * dva:(g + 1) * dva, pl.ds(ks, KEY_TILE)], m_ref, None, acc_ref, g)

    def far_pair(u, carry):
        tile(2 * u, False)
        tile(2 * u + 1, False)
        return carry

    def far_body(t, carry):
        tile(t, False)
        return carry

    def near_body(t, carry):
        tile(t, True)
        return carry

    lax.fori_loop(0, n_far // 2, far_pair, 0)
    lax.fori_loop(n_far // 2 * 2, n_far, far_body, 0)
    lax.fori_loop(n_far, n_valid, near_body, 0)
    for g in range(kv_heads):
        o_t = acc_ref[g, :dh] / acc_ref[g, dh:dh + 1]
        for r in range(group):
            h = group * g + r
            o_ref[:, h * dh:(h + 1) * dh] = o_t[:, r * tq:(r + 1) * tq].T.astype(BF)


def _dsa_sample_kernel(q_ref, qi_ref, wi_ref, kdc_ref, kdn_ref, kn_ref, vn_ref, tab_ref, kc_ref, vc_ref, dst_ref,
                       o_ref, key_ref, qim_ref, *, t, past, topk, kv_heads, group, idx_dim):
    del dst_ref
    dh = LANES
    idx_heads = qim_ref.shape[0] // t

    def cached(ref, g, c):
        return ref[pl.ds(c * KEY_TILE * kv_heads + g, KEY_TILE, stride=kv_heads), :].astype(BF)

    qb = past // LANES
    n_c = past // KEY_TILE
    s_real = past + t
    qchunk = (past + lax.broadcasted_iota(jnp.int32, (t, 1), 0)) >> CHUNK_SHIFT

    lane = lax.broadcasted_iota(jnp.int32, (t, LANES), 1)
    per_pair = LANES // idx_dim
    wi = wi_ref[...]
    w_rows = []
    for h in range(idx_heads):
        pair = qi_ref[:, (h // per_pair) * LANES:(h // per_pair + 1) * LANES].astype(F32)
        sub = h % per_pair
        keep = (lane >= sub * idx_dim) & (lane < (sub + 1) * idx_dim)
        qim_ref[h * t:(h + 1) * t, :] = jnp.where(keep, pair, 0.0).astype(BF)
        w_rows.append(jnp.broadcast_to(wi[:, h:h + 1], (t, LANES)))
    w_rows = jnp.concatenate(w_rows, axis=0)

    def keys_of(kd, kpos):
        w = kd.shape[0]
        sc = jnp.maximum(_dot_nt(qim_ref[...], kd), 0.0) * jnp.concatenate([w_rows] * (w // LANES), axis=1)
        acc = jnp.sum(sc.reshape(idx_heads, t, w), axis=0)
        valid = ((kpos >> CHUNK_SHIFT) <= qchunk) & (kpos < s_real)
        bits = lax.bitcast_convert_type(acc, jnp.int32)
        key = bits ^ ((bits >> 31) & 0x7FFFFFFF)
        return jnp.where(valid, key, INT_MIN)

    for c in range(n_c):
        kpos = c * KEY_TILE + lax.broadcasted_iota(jnp.int32, (t, KEY_TILE), 1)
        key_ref[:, c * KEY_TILE:(c + 1) * KEY_TILE] = keys_of(kdc_ref[c * KEY_TILE:(c + 1) * KEY_TILE, :], kpos)
    kpos = past + lax.broadcasted_iota(jnp.int32, (t, LANES), 1)
    key_ref[:, past:] = keys_of(kdn_ref[...].astype(BF), kpos)

    def count_ge(cand):
        return jnp.sum(jnp.where(key_ref[...] >= cand, 1.0, 0.0), axis=-1, keepdims=True)

    thr = _topk_threshold(count_ge, (t, 1), topk)
    mb = jnp.where(key_ref[...] >= thr, 0.0, NEG)

    for g in range(kv_heads):
        hs = slice(group * g, group * (g + 1))
        qg = jnp.concatenate(
            [q_ref[:, (group * g + r) * dh:(group * g + r + 1) * dh] for r in range(group)], axis=0)
        parts = []
        for c in range(n_c):
            s = _dot_nt(qg, cached(kc_ref, g, c)).reshape(group, t, KEY_TILE)
            v = qb - 2 * c
            if v <= 2:
                s = s + tab_ref[v, hs]
            parts.append(s)
        s = _dot_nt(qg, kn_ref[:, g * dh:(g + 1) * dh]).reshape(group, t, LANES)
        parts.append(s + tab_ref[0, hs, :, :LANES])
        s_all = jnp.concatenate(parts, axis=-1) + mb[None]
        m = jnp.max(s_all, axis=-1, keepdims=True)
        p = jnp.exp2(s_all - m)
        l = jnp.sum(p, axis=-1, keepdims=True)
        pb = p.astype(BF).reshape(group * t, past + LANES)
        acc = _dot(pb[:, past:], vn_ref[:, g * dh:(g + 1) * dh])
        for c in range(n_c):
            acc = acc + _dot(pb[:, c * KEY_TILE:(c + 1) * KEY_TILE], cached(vc_ref, g, c))
        out = acc.reshape(group, t, dh) / l
        for r in range(group):
            o_ref[:, (group * g + r) * dh:(group * g + r + 1) * dh] = out[r].astype(BF)


def _dsa_prompt(qkv, small, vt, tab_t, dst, *, batch, seq, heads, kv_heads, idx_heads, idx_dim):
    tq = LANES
    assert seq % KEY_TILE == 0
    nt = seq // tq
    dh = LANES
    group = heads // kv_heads
    qcols, kvcols, qicols = heads * dh, kv_heads * dh, idx_heads * idx_dim
    kern = functools.partial(_dsa_prompt_kernel, tq=tq, topk=min(TOPK_MAX, seq // 4),
                             kv_heads=kv_heads, group=group, idx_dim=idx_dim)
    return pl.pallas_call(
        kern,
        out_shape=jax.ShapeDtypeStruct(dst.shape, dst.dtype),
        grid=(batch, nt),
        in_specs=[
            pl.BlockSpec((tq, qcols), lambda b, i: (b * nt + i, 0)),
            pl.BlockSpec((tq, qicols), lambda b, i: (b * nt + i, (qcols + 2 * kvcols) // qicols)),
            pl.BlockSpec((tq, LANES), lambda b, i: (b * nt + i, 1)),
            pl.BlockSpec((seq, kvcols), lambda b, i: (b, qcols // kvcols)),
            pl.BlockSpec((vt.shape[0], seq), lambda b, i: (0, b)),
            pl.BlockSpec((seq, LANES), lambda b, i: (b, 0)),
            pl.BlockSpec((3, kv_heads, KEY_TILE, group * tq), lambda b, i: (0, 0, 0, 0)),
            pl.BlockSpec(memory_space=pl.ANY),
        ],
        input_output_aliases={7: 0},
        out_specs=pl.BlockSpec((tq, qcols), lambda b, i: (b * nt + i, 0)),
        scratch_shapes=[pltpu.VMEM((seq, tq), jnp.int32),
                        pltpu.VMEM((seq, tq), F32),
                        pltpu.VMEM((idx_heads * idx_dim // LANES, 2 * tq, LANES), BF),
                        pltpu.VMEM((kv_heads, group * tq, dh), BF),
                        pltpu.VMEM((kv_heads, 1, group * tq), F32),
                        pltpu.VMEM((kv_heads, vt.shape[0] // kv_heads, group * tq), F32)],
        compiler_params=_cparams("parallel", "arbitrary"),
        name="dsa_prompt",
    )(qkv, qkv, small, qkv, vt, small, tab_t, dst)


def _dsa_sample(qkv, small, kd_cache, kd_new, k_new, v_new, cache_k, cache_v, tab, dst, *, layer, row0, streams, t,
                heads, kv_heads, idx_heads, idx_dim):
    n_layers, _, past = cache_k.shape[:3]
    dh = LANES
    group = heads // kv_heads
    qcols, kvcols, qicols = heads * dh, kv_heads * dh, idx_heads * idx_dim
    assert past % KEY_TILE == 0 and t <= LANES and row0 % t == 0
    rb0 = row0 // t
    kern = functools.partial(_dsa_sample_kernel, t=t, past=past, topk=min(TOPK_MAX, (past + t) // 4),
                             kv_heads=kv_heads, group=group, idx_dim=idx_dim)
    rows = lambda c: c.reshape(n_layers, streams, past * kv_heads, dh)
    cache_spec = pl.BlockSpec((None, None, past * kv_heads, dh), lambda b: (layer, b, 0, 0))

    return pl.pallas_call(
        kern,
        out_shape=jax.ShapeDtypeStruct(dst.shape, dst.dtype),
        grid=(streams,),
        in_specs=[
            pl.BlockSpec((t, qcols), lambda b: (rb0 + b, 0)),
            pl.BlockSpec((t, qicols), lambda b: (rb0 + b, (qcols + 2 * kvcols) // qicols)),
            pl.BlockSpec((t, LANES), lambda b: (rb0 + b, 1)),
            pl.BlockSpec((past, LANES), lambda b: (b, 0)),
            pl.BlockSpec((LANES, LANES), lambda b: (b, 0)),
            pl.BlockSpec((LANES, kvcols), lambda b: (b, 0)),
            pl.BlockSpec((LANES, kvcols), lambda b: (b, 0)),
            pl.BlockSpec((3, heads, t, KEY_TILE), lambda b: (0, 0, 0, 0)),
            cache_spec,
            cache_spec,
            pl.BlockSpec(memory_space=pl.ANY),
        ],
        out_specs=pl.BlockSpec((t, qcols), lambda b: (rb0 + b, 0)),
        input_output_aliases={10: 0},
        scratch_shapes=[pltpu.VMEM((t, past + LANES), jnp.int32),
                        pltpu.VMEM((idx_heads * t, LANES), BF)],
        compiler_params=_cparams("parallel"),
        name="dsa_sample",
    )(qkv, qkv, small, kd_cache, kd_new, k_new, v_new, tab, rows(cache_k), rows(cache_v), dst)


def _rope_tables(pos, rope):
    half = rope // 2
    inv_freq = ROPE_THETA ** (-jnp.arange(half, dtype=F32) / half)
    ang = pos.astype(F32)[:, None] * inv_freq[None, :]
    cos, sin = jnp.cos(ang), jnp.sin(ang)
    pad = jnp.zeros((pos.shape[0], LANES - rope), F32)
    return (jnp.concatenate([cos, cos, pad], axis=1), jnp.concatenate([sin, sin, pad], axis=1))


@jax.jit
def _step(x_prompt, x_sample, cache_mla_latent, cache_mla_krope, cache_dsa_k, cache_dsa_v, cache_dsa_kidx,
          norm_g, norm_f, ffn_w_in, ffn_w_out, mla_w_down, mla_g_q, mla_g_kv, mla_w_uq, mla_w_uk, mla_w_uv,
          mla_w_o, dsa_w_in, dsa_g_kidx, dsa_w_o, rel_bias):
    batch, seq, d_model = x_prompt.shape
    streams, dec, _ = x_sample.shape
    depth = norm_g.shape[0]
    past = cache_mla_latent.shape[2]
    kv_lora = cache_mla_latent.shape[3]
    rope = cache_mla_krope.shape[3]
    q_lora = mla_w_uq.shape[1]
    mla_heads = mla_w_uq.shape[2]
    nope = mla_w_uk.shape[3]
    kv_heads, dh = cache_dsa_k.shape[3], cache_dsa_k.shape[4]
    idx_dim = cache_dsa_kidx.shape[3]
    dsa_heads = dsa_w_o.shape[1] // dh
    idx_heads = rel_bias.shape[1]
    assert nope == LANES and dh == LANES and 2 * rope == LANES and mla_w_uv.shape[3] == LANES
    assert 2 * idx_dim == LANES and idx_heads <= LANES
    mp, ms = batch * seq, streams * dec
    m = mp + ms
    tm = m // 8
    assert tm * 8 == m and tm % 16 == 0
    qcols, kvcols, qicols = dsa_heads * dh, kv_heads * dh, idx_heads * idx_dim
    big = qcols + 2 * kvcols + qicols

    x = jnp.concatenate([x_prompt.reshape(mp, d_model), x_sample.reshape(ms, d_model)], axis=0)
    g3 = norm_g.reshape(depth * 3, 1, d_model)
    pos = jnp.concatenate([jnp.tile(jnp.arange(seq, dtype=jnp.int32), batch),
                           jnp.tile(past + jnp.arange(dec, dtype=jnp.int32), streams)])
    cos_t, sin_t = _rope_tables(pos, rope)
    wd_t = jnp.transpose(mla_w_down, (0, 2, 1))
    dw_t = jnp.transpose(dsa_w_in, (0, 2, 1))
    wq_t = jnp.transpose(mla_w_uq, (0, 2, 3, 1))
    o_init = jnp.zeros((m, d_model), BF)
    tab = _bias_tab(rel_bias, dec)
    tab_t = _bias_tab_t(rel_bias, kv_heads, LANES)

    def ffn(x, li, s):
        act = _ffn_in(x, g3, li * 3 + 2 * s, ffn_w_in, (li, s), tm, 512)
        return _mm_res(act, ffn_w_out, (li, s), x, 0.5, tm, 256)

    def new_slab(a):
        a = a.reshape(streams, dec, a.shape[-1])
        return jnp.pad(a, ((0, 0), (0, LANES - dec), (0, 0))).reshape(streams * LANES, a.shape[-1])

    new_c, new_kr, new_k, new_v, new_ki = [], [], [], [], []
    for li in range(depth):
        j = li // 2
        gi = li * 3 + 1
        x = ffn(x, li, 0)
        if li % 2 == 0:
            w_kr_t = jnp.pad(wd_t[j, q_lora + kv_lora:], ((0, LANES - rope), (0, 0)))[None]
            d = _norm_mm(x, g3, gi, wd_t, (j,), 0, (q_lora + kv_lora) // 1024, 1024, F32, tm)
            kr2 = _norm_mm(x, g3, gi, w_kr_t, (0,), 0, 1, LANES, F32, tm)
            cq, ckv, aug, kr = _mla_post(d, kr2, mla_g_q[j], mla_g_kv[j], cos_t, sin_t, tm // 2,
                                         q_lora, kv_lora, rope)
            new_c.append(ckv)
            new_kr.append(kr)
            scale = (nope + rope) ** -0.5 * LOG2E
            q_arr = _mla_qup(cq, wq_t, j, cos_t, sin_t, tm, rope, scale, 4)
            w_kexp = jnp.zeros((kv_lora + LANES, mla_heads, 2 * LANES), F32)
            w_kexp = w_kexp.at[:kv_lora, :, :nope].set(mla_w_uk[j])
            w_kexp = w_kexp.at[kv_lora:kv_lora + rope, :, nope:nope + rope].set(
                jnp.broadcast_to(jnp.eye(rope, dtype=F32)[:, None, :], (rope, mla_heads, rope)))
            w_kexp = w_kexp.reshape(kv_lora + LANES, mla_heads * 2 * LANES).astype(BF)
            k_arr = _mm(aug, w_kexp, mp, kv_lora + LANES, 1024, 2048, BF)
            wuv_rows = jnp.transpose(mla_w_uv[j], (1, 2, 0)).reshape(mla_heads * LANES, kv_lora).astype(BF)
            vt_arr = _mm_nt(wuv_rows, aug, mp, kv_lora, 512, 1024, BF)
            o = _mla_attn_prompt(q_arr, k_arr, vt_arr, o_init, batch, seq, mla_heads, 512, 4)
            wuk_t = jnp.transpose(mla_w_uk[j], (1, 0, 2)).astype(BF)
            wuv_t = jnp.transpose(mla_w_uv[j], (1, 0, 2)).astype(BF)
            q_lat, q_rope = _mla_qlat(q_arr, wuk_t, mp, ms)
            kr_pad = jnp.pad(cache_mla_krope.astype(BF), ((0, 0), (0, 0), (0, 0), (0, LANES - rope)))
            o_lat = _mla_attn_sample(q_lat, q_rope, cache_mla_latent, kr_pad, aug, j, mp, streams, dec)
            o = _mla_oup(o_lat, wuv_t, o, mp)
            x = _mm_res(o, mla_w_o, (j,), x, 1.0, tm, 512)
        else:
            assert 2 * kvcols == 1024 and qcols % 1024 == 0
            qkv, kv32 = _norm_mm(x, g3, gi, dw_t, (j,), 0, big // 1024, 1024, BF, tm,
                                 scale_blocks=(qcols // 1024, dh ** -0.5 * LOG2E), keep_blk=qcols // 1024)
            w_ki = dw_t[j, big:big + idx_dim]
            w_wi = dw_t[j, big + idx_dim:]
            w_small = jnp.concatenate([w_ki, w_ki, w_wi, jnp.zeros((LANES - idx_heads, d_model), F32)], axis=0)
            gk2 = jnp.tile(dsa_g_kidx[j], 2).reshape(1, LANES)
            small = _dsa_small(x, g3, gi, w_small, gk2, tm, idx_dim, (idx_dim ** -0.5) * (idx_heads ** -0.5))
            new_k.append(kv32[:, :kvcols])
            new_v.append(kv32[:, kvcols:])
            new_ki.append(small[:, :idx_dim])
            common = dict(heads=dsa_heads, kv_heads=kv_heads, idx_heads=idx_heads, idx_dim=idx_dim)
            vt = qkv[:mp, qcols + kvcols:qcols + 2 * kvcols].T.reshape(kv_heads, dh, mp)
            vt = jnp.concatenate([vt, jnp.ones((kv_heads, 16, mp), BF)], axis=1).reshape(kv_heads * (dh + 16), mp)
            o = _dsa_prompt(qkv, small, vt, tab_t, o_init, batch=batch, seq=seq, **common)
            ki_c = cache_dsa_kidx[j].astype(BF).reshape(streams * past, idx_dim)
            o = _dsa_sample(qkv, small, jnp.concatenate([ki_c, ki_c], axis=-1), new_slab(small[mp:, :LANES]),
                            new_slab(qkv[mp:, qcols:qcols + kvcols]),
                            new_slab(qkv[mp:, qcols + kvcols:qcols + 2 * kvcols]),
                            cache_dsa_k, cache_dsa_v, tab, o, layer=j, row0=mp, streams=streams, t=dec, **common)
            x = _mm_res(o, dsa_w_o, (j,), x, 1.0, tm, 512)
        x = ffn(x, li, 1)

    tf = math.gcd(mp, ms)
    y_p = _final_norm(x, norm_f, 0, mp, tf)
    y_s = _final_norm(x, norm_f, mp, ms, tf)

    def split(rows, *tail):
        a = jnp.stack(rows)
        n = a.shape[0]
        return (a[:, :mp].reshape(n, batch, seq, *tail), a[:, mp:].reshape(n, streams, dec, *tail))

    p_c, s_c = split(new_c, kv_lora)
    p_kr, s_kr = split(new_kr, rope)
    p_k, s_k = split(new_k, kv_heads, dh)
    p_v, s_v = split(new_v, kv_heads, dh)
    p_ki, s_ki = split(new_ki, idx_dim)
    return (y_p.reshape(batch, seq, d_model), y_s.reshape(streams, dec, d_model),
            p_c, p_kr, p_k, p_v, p_ki, s_c, s_kr, s_k, s_v, s_ki)


def kernel(x_prompt, x_sample, cache_mla_latent, cache_mla_krope, cache_dsa_k, cache_dsa_v, cache_dsa_kidx,
           norm_g, norm_f, ffn_w_in, ffn_w_out, mla_w_down, mla_g_q, mla_g_kv, mla_w_uq, mla_w_uk, mla_w_uv,
           mla_w_o, dsa_w_in, dsa_g_kidx, dsa_w_o, rel_bias):
    return _step(x_prompt, x_sample, cache_mla_latent, cache_mla_krope, cache_dsa_k, cache_dsa_v,
                 cache_dsa_kidx, norm_g, norm_f, ffn_w_in, ffn_w_out, mla_w_down, mla_g_q, mla_g_kv,
                 mla_w_uq, mla_w_uk, mla_w_uv, mla_w_o, dsa_w_in, dsa_g_kidx, dsa_w_o, rel_bias)
```

```python
import functools
import math

import jax
import jax.numpy as jnp
from jax import lax
from jax.experimental import pallas as pl
from jax.experimental.pallas import tpu as pltpu

BF = jnp.bfloat16
F32 = jnp.float32

CHUNK = 64
ROPE_THETA = 10000.0
NORM_EPS = 1e-6
TOPK_MAX = 256
REL_BUCKETS = 32
REL_MAX_DIST = 128

LANES = 128
KEY_TILE = 2 * LANES
VMEM_LIMIT = 56 * 1024 * 1024
NEG = -1e30
INT_MIN = -(2 ** 31)
LOG2E = math.log2(math.e)
CHUNK_SHIFT = CHUNK.bit_length() - 1
assert 1 << CHUNK_SHIFT == CHUNK

NT = (((1,), (1,)), ((), ()))


def _cparams(*sem):
    return pltpu.CompilerParams(dimension_semantics=sem, vmem_limit_bytes=VMEM_LIMIT)


def _rms(x, g):
    ms = jnp.mean(x * x, axis=-1, keepdims=True)
    return x * lax.rsqrt(ms + NORM_EPS) * g


def _dot(a, b):
    return jnp.dot(a, b, preferred_element_type=F32)


def _dot_nt(a, b):
    return lax.dot_general(a, b, NT, preferred_element_type=F32)


def _wspec(prefix, k, tn, col0=0):
    return pl.BlockSpec((None,) * len(prefix) + (k, tn),
                        lambda i, j: tuple(prefix) + (0, col0 + j))


def _ffn_in_kernel(x_ref, g_ref, wg_ref, wu_ref, o_ref, h_ref):
    @pl.when(pl.program_id(1) == 0)
    def _():
        h_ref[...] = _rms(x_ref[...], g_ref[...]).astype(BF)

    kc = 512
    gate = up = None
    for k0 in range(0, h_ref.shape[1], kc):
        h = h_ref[:, k0:k0 + kc]
        g = _dot(h, wg_ref[k0:k0 + kc, :].astype(BF))
        u = _dot(h, wu_ref[k0:k0 + kc, :].astype(BF))
        gate = g if gate is None else gate + g
        up = u if up is None else up + u
    o_ref[...] = (gate / (1.0 + jnp.exp(-gate)) * up).astype(BF)


def _ffn_in(x, g3, gi, w_in, prefix, tm, tf):
    m, d = x.shape
    d_ff = w_in.shape[-1] // 2
    nf = d_ff // tf
    return pl.pallas_call(
        _ffn_in_kernel,
        out_shape=jax.ShapeDtypeStruct((m, d_ff), BF),
        grid=(m // tm, nf),
        in_specs=[
            pl.BlockSpec((tm, d), lambda i, j: (i, 0)),
            pl.BlockSpec((None, 1, d), lambda i, j: (gi, 0, 0)),
            _wspec(prefix, d, tf),
            _wspec(prefix, d, tf, nf),
        ],
        out_specs=pl.BlockSpec((tm, tf), lambda i, j: (i, j)),
        scratch_shapes=[pltpu.VMEM((tm, d), BF)],
        compiler_params=_cparams("parallel", "arbitrary"),
        name="ffn_in",
    )(x, g3, w_in, w_in)


def _mm_res_kernel(a_ref, w_ref, x_ref, o_ref, *, scale):
    acc = _dot(a_ref[...], w_ref[...].astype(BF))
    o_ref[...] = x_ref[...] + scale * acc


def _mm_res(a, w, prefix, x, scale, tm, tn):
    m, k = a.shape
    n = x.shape[1]
    return pl.pallas_call(
        functools.partial(_mm_res_kernel, scale=scale),
        out_shape=jax.ShapeDtypeStruct((m, n), F32),
        grid=(m // tm, n // tn),
        in_specs=[
            pl.BlockSpec((tm, k), lambda i, j: (i, 0)),
            _wspec(prefix, k, tn),
            pl.BlockSpec((tm, tn), lambda i, j: (i, j)),
        ],
        out_specs=pl.BlockSpec((tm, tn), lambda i, j: (i, j)),
        compiler_params=_cparams("parallel", "arbitrary"),
        name="mm_res",
    )(a, w, x)


def _norm_mm_kernel(x_ref, g_ref, w_ref, o_ref, *rest, scale_blocks, keep_blk):
    h_ref = rest[-1]
    j = pl.program_id(1)

    @pl.when(j == 0)
    def _():
        h_ref[...] = _rms(x_ref[...], g_ref[...]).astype(BF)

    kc = 512
    acc = None
    for k0 in range(0, h_ref.shape[1], kc):
        part = _dot_nt(h_ref[:, k0:k0 + kc], w_ref[:, k0:k0 + kc].astype(BF))
        acc = part if acc is None else acc + part
    if keep_blk is not None:
        @pl.when(j == keep_blk)
        def _():
            rest[0][...] = acc
    if scale_blocks is not None:
        nb, val = scale_blocks
        acc = acc * jnp.where(j < nb, val, 1.0).astype(F32)
    o_ref[...] = acc.astype(o_ref.dtype)


def _norm_mm(x, g3, gi, w_t, prefix, row0_blk, n_blk, tn, out_dtype, tm, scale_blocks=None, keep_blk=None):
    m, d = x.shape
    out_shape = [jax.ShapeDtypeStruct((m, n_blk * tn), out_dtype)]
    out_specs = [pl.BlockSpec((tm, tn), lambda i, j: (i, j))]
    if keep_blk is not None:
        out_shape.append(jax.ShapeDtypeStruct((m, tn), F32))
        out_specs.append(pl.BlockSpec((tm, tn), lambda i, j: (i, 0)))
    out = pl.pallas_call(
        functools.partial(_norm_mm_kernel, scale_blocks=scale_blocks, keep_blk=keep_blk),
        out_shape=out_shape,
        grid=(m // tm, n_blk),
        in_specs=[
            pl.BlockSpec((tm, d), lambda i, j: (i, 0), pipeline_mode=pl.Buffered(1)),
            pl.BlockSpec((None, 1, d), lambda i, j: (gi, 0, 0)),
            pl.BlockSpec((None,) * len(prefix) + (tn, d), lambda i, j: tuple(prefix) + (row0_blk + j, 0)),
        ],
        out_specs=out_specs,
        scratch_shapes=[pltpu.VMEM((tm, d), BF)],
        compiler_params=_cparams("parallel", "arbitrary"),
        name="norm_mm",
    )(x, g3, w_t)
    return out if keep_blk is not None else out[0]


def _mm_kernel(a_ref, w_ref, o_ref):
    o_ref[...] = _dot(a_ref[...], w_ref[...].astype(BF)).astype(o_ref.dtype)


def _mm(a, w, m_rows, k_cols, tm, tn, out_dtype):
    n = w.shape[1]
    return pl.pallas_call(
        _mm_kernel,
        out_shape=jax.ShapeDtypeStruct((m_rows, n), out_dtype),
        grid=(m_rows // tm, n // tn),
        in_specs=[
            pl.BlockSpec((tm, k_cols), lambda i, j: (i, 0)),
            pl.BlockSpec((k_cols, tn), lambda i, j: (0, j)),
        ],
        out_specs=pl.BlockSpec((tm, tn), lambda i, j: (i, j)),
        compiler_params=_cparams("parallel", "arbitrary"),
        name="mm",
    )(a, w)


def _mm_nt_kernel(w_ref, a_ref, o_ref):
    o_ref[...] = _dot_nt(w_ref[...], a_ref[...]).astype(o_ref.dtype)


def _mm_nt(w_t, a, m_rows, k_cols, tn_rows, tm_cols, out_dtype):
    n = w_t.shape[0]
    return pl.pallas_call(
        _mm_nt_kernel,
        out_shape=jax.ShapeDtypeStruct((n, m_rows), out_dtype),
        grid=(n // tn_rows, m_rows // tm_cols),
        in_specs=[
            pl.BlockSpec((tn_rows, k_cols), lambda i, j: (i, 0)),
            pl.BlockSpec((tm_cols, k_cols), lambda i, j: (j, 0)),
        ],
        out_specs=pl.BlockSpec((tn_rows, tm_cols), lambda i, j: (i, j)),
        compiler_params=_cparams("parallel", "arbitrary"),
        name="mm_nt",
    )(w_t, a)


def _final_norm_kernel(x_ref, g_ref, o_ref):
    o_ref[...] = _rms(x_ref[...], g_ref[...])


def _final_norm(x, g, row0, rows, tm):
    d = x.shape[1]
    assert row0 % tm == 0 and rows % tm == 0
    return pl.pallas_call(
        _final_norm_kernel,
        out_shape=jax.ShapeDtypeStruct((rows, d), F32),
        grid=(rows // tm,),
        in_specs=[pl.BlockSpec((tm, d), lambda i: (row0 // tm + i, 0)),
                  pl.BlockSpec((1, d), lambda i: (0, 0))],
        out_specs=pl.BlockSpec((tm, d), lambda i: (i, 0)),
        compiler_params=_cparams("parallel"),
        name="final_norm",
    )(x, g.reshape(1, d))


def _rope_rotate(x, cos_t, sin_t, rope):
    half = rope // 2
    lane = lax.broadcasted_iota(jnp.int32, x.shape, 1)
    rot_half = jnp.where(lane < half, -pltpu.roll(x, LANES - half, 1), pltpu.roll(x, half, 1))
    return x * cos_t + rot_half * sin_t


def _mla_post_kernel(d_ref, kr2_ref, gq_ref, gkv_ref, c_ref, s_ref,
                     cq_ref, ckv_ref, aug_ref, kr_ref, *, q_lora, kv_lora, rope):
    d = d_ref[...]
    cq_ref[...] = _rms(d[:, :q_lora], gq_ref[...]).astype(BF)
    ckv = _rms(d[:, q_lora:q_lora + kv_lora], gkv_ref[...])
    ckv_ref[...] = ckv
    kr = _rope_rotate(kr2_ref[...], c_ref[...], s_ref[...], rope)
    aug_ref[:, :kv_lora] = ckv.astype(BF)
    aug_ref[:, kv_lora:] = kr.astype(BF)
    kr_ref[...] = kr[:, :rope]


def _mla_post(d, kr2, g_q, g_kv, cos_t, sin_t, tm, q_lora, kv_lora, rope):
    m = d.shape[0]
    row = lambda w: pl.BlockSpec((tm, w), lambda i: (i, 0))
    vec = lambda w: pl.BlockSpec((1, w), lambda i: (0, 0))
    return pl.pallas_call(
        functools.partial(_mla_post_kernel, q_lora=q_lora, kv_lora=kv_lora, rope=rope),
        out_shape=(jax.ShapeDtypeStruct((m, q_lora), BF),
                   jax.ShapeDtypeStruct((m, kv_lora), F32),
                   jax.ShapeDtypeStruct((m, kv_lora + LANES), BF),
                   jax.ShapeDtypeStruct((m, rope), F32)),
        grid=(m // tm,),
        in_specs=[row(d.shape[1]), row(LANES), vec(q_lora), vec(kv_lora), row(LANES), row(LANES)],
        out_specs=(row(q_lora), row(kv_lora), row(kv_lora + LANES), row(rope)),
        compiler_params=_cparams("parallel"),
        name="mla_post",
    )(d, kr2, g_q.reshape(1, -1), g_kv.reshape(1, -1), cos_t, sin_t)


def _mla_qup_kernel(cq_ref, w_ref, c_ref, s_ref, o_ref, *, rope, scale, hq):
    cq = cq_ref[...]
    k = cq.shape[1]
    for h in range(hq):
        w = jnp.concatenate([w_ref[h], jnp.zeros((2 * LANES - w_ref.shape[1], k), F32)], axis=0)
        acc = _dot_nt(cq, w.astype(BF))
        lo = acc[:, :LANES]
        rot = _rope_rotate(acc[:, LANES:], c_ref[...], s_ref[...], rope)
        o_ref[:, 2 * h * LANES:(2 * h + 1) * LANES] = (lo * scale).astype(BF)
        o_ref[:, (2 * h + 1) * LANES:(2 * h + 2) * LANES] = (rot * scale).astype(BF)


def _mla_qup(cq, wq_t, layer, cos_t, sin_t, tm, rope, scale, hq):
    m, k = cq.shape
    heads, hd = wq_t.shape[1], wq_t.shape[2]
    return pl.pallas_call(
        functools.partial(_mla_qup_kernel, rope=rope, scale=scale, hq=hq),
        out_shape=jax.ShapeDtypeStruct((m, heads * 2 * LANES), BF),
        grid=(m // tm, heads // hq),
        in_specs=[
            pl.BlockSpec((tm, k), lambda i, h: (i, 0)),
            pl.BlockSpec((None, hq, hd, k), lambda i, h: (layer, h, 0, 0)),
            pl.BlockSpec((tm, LANES), lambda i, h: (i, 0)),
            pl.BlockSpec((tm, LANES), lambda i, h: (i, 0)),
        ],
        out_specs=pl.BlockSpec((tm, hq * 2 * LANES), lambda i, h: (i, h)),
        compiler_params=_cparams("parallel", "arbitrary"),
        name="mla_qup",
    )(cq, wq_t, cos_t, sin_t)


def _online_update_t(s_t, vt_tile, m_ref, l_ref, acc_ref, hs):
    m_old = m_ref[hs]
    m_new = jnp.maximum(m_old, jnp.max(s_t, axis=0, keepdims=True))
    alpha = jnp.exp2(m_old - m_new)
    p = jnp.exp2(s_t - m_new)
    if l_ref is not None:
        l_ref[hs] = alpha * l_ref[hs] + jnp.sum(p, axis=0, keepdims=True)
    acc_ref[hs] = alpha * acc_ref[hs] + _dot(vt_tile, p.astype(BF))
    m_ref[hs] = m_new


def _mla_attn_prompt_kernel(q_ref, k_ref, vt_ref, dst_ref, o_ref, m_ref, l_ref, acc_ref, *, tq, hp):
    del dst_ref
    i = pl.program_id(2)
    dk, dv = 2 * LANES, LANES
    m_ref[...] = jnp.full(m_ref.shape, NEG, F32)
    l_ref[...] = jnp.zeros(l_ref.shape, F32)
    acc_ref[...] = jnp.zeros(acc_ref.shape, F32)

    def tile(t, diagonal):
        ks = pl.multiple_of(t * tq, tq)

        def scores(h):
            s_t = _dot_nt(k_ref[pl.ds(ks, tq), h * dk:(h + 1) * dk], q_ref[:, h * dk:(h + 1) * dk])
            if diagonal:
                kc = lax.broadcasted_iota(jnp.int32, (tq, 1), 0) >> CHUNK_SHIFT
                qc = lax.broadcasted_iota(jnp.int32, (1, tq), 1) >> CHUNK_SHIFT
                s_t = jnp.where(kc <= qc, s_t, NEG)
            return s_t

        s_next = scores(0)
        for h in range(hp):
            s_t = s_next
            if h + 1 < hp:
                s_next = scores(h + 1)
            _online_update_t(s_t, vt_ref[h * dv:(h + 1) * dv, pl.ds(ks, tq)], m_ref, l_ref, acc_ref, h)

    def pair(u, carry):
        tile(2 * u, False)
        tile(2 * u + 1, False)
        return carry

    def body(t, carry):
        tile(t, False)
        return carry

    lax.fori_loop(0, i // 2, pair, 0)
    lax.fori_loop(i // 2 * 2, i, body, 0)
    tile(i, True)
    for h in range(hp):
        o_ref[:, h * dv:(h + 1) * dv] = (acc_ref[h] / l_ref[h]).T.astype(BF)


def _mla_attn_prompt(q_arr, k_arr, vt_arr, dst, batch, seq, heads, tq, hp):
    assert tq % CHUNK == 0 and seq % tq == 0 and heads % hp == 0
    nt = seq // tq
    dk, dv = 2 * LANES, LANES
    return pl.pallas_call(
        functools.partial(_mla_attn_prompt_kernel, tq=tq, hp=hp),
        out_shape=jax.ShapeDtypeStruct(dst.shape, dst.dtype),
        grid=(batch, heads // hp, nt),
        in_specs=[
            pl.BlockSpec((tq, hp * dk), lambda b, h, i: (b * nt + i, h)),
            pl.BlockSpec((seq, hp * dk), lambda b, h, i: (b, h)),
            pl.BlockSpec((hp * dv, seq), lambda b, h, i: (h, b)),
            pl.BlockSpec(memory_space=pl.ANY),
        ],
        input_output_aliases={3: 0},
        out_specs=pl.BlockSpec((tq, hp * dv), lambda b, h, i: (b * nt + i, h)),
        scratch_shapes=[pltpu.VMEM((hp, 1, tq), F32), pltpu.VMEM((hp, 1, tq), F32),
                        pltpu.VMEM((hp, dv, tq), F32)],
        compiler_params=_cparams("parallel", "parallel", "arbitrary"),
        name="mla_attn_prompt",
    )(q_arr, k_arr, vt_arr, dst)


def _mla_qlat_kernel(qn_ref, qr_ref, wuk_ref, ql_ref, qro_ref):
    ql_ref[...] = _dot_nt(qn_ref[...], wuk_ref[...]).astype(BF)
    qro_ref[...] = qr_ref[...]


def _mla_qlat(q_arr, wuk_t, row0, rows):
    heads, kv_lora, _ = wuk_t.shape
    rb = row0 // rows
    return pl.pallas_call(
        _mla_qlat_kernel,
        out_shape=(jax.ShapeDtypeStruct((heads, rows, kv_lora), BF),
                   jax.ShapeDtypeStruct((heads, rows, LANES), BF)),
        grid=(heads,),
        in_specs=[
            pl.BlockSpec((rows, LANES), lambda h: (rb, 2 * h)),
            pl.BlockSpec((rows, LANES), lambda h: (rb, 2 * h + 1)),
            pl.BlockSpec((None, kv_lora, LANES), lambda h: (h, 0, 0)),
        ],
        out_specs=(pl.BlockSpec((None, rows, kv_lora), lambda h: (h, 0, 0)),
                   pl.BlockSpec((None, rows, LANES), lambda h: (h, 0, 0))),
        compiler_params=_cparams("parallel"),
        name="mla_qlat",
    )(q_arr, q_arr, wuk_t)


def _mla_attn_sample_kernel(ql_ref, qr_ref, c_ref, kr_ref, aug_ref, o_ref, *, heads, t, kv_lora, per_blk):
    b = pl.program_id(0)
    rows = heads * t
    ql = ql_ref[...].reshape(rows, kv_lora)
    qr = qr_ref[...].reshape(rows, LANES)
    c = c_ref[...].astype(BF)
    kr_t = kr_ref[...]
    kr_t = jnp.concatenate([kr_t, jnp.zeros((LANES - kr_t.shape[0], kr_t.shape[1]), F32)], axis=0).astype(BF)
    s_c = _dot_nt(ql, c) + _dot(qr, kr_t)
    cn = aug_ref[:, :kv_lora]
    s_n = _dot_nt(ql, cn) + _dot_nt(qr, aug_ref[:, kv_lora:])
    owner = lax.broadcasted_iota(jnp.int32, (1, LANES), 1) // t
    s_n = jnp.where(owner == b % per_blk, s_n, NEG)
    m = jnp.maximum(jnp.max(s_c, axis=-1, keepdims=True), jnp.max(s_n, axis=-1, keepdims=True))
    p_c = jnp.exp2(s_c - m)
    p_n = jnp.exp2(s_n - m)
    l = jnp.sum(p_c, axis=-1, keepdims=True) + jnp.sum(p_n, axis=-1, keepdims=True)
    o = (_dot(p_c.astype(BF), c) + _dot(p_n.astype(BF), cn)) / l
    o_ref[...] = o.reshape(heads, t, kv_lora).astype(BF)


def _mla_attn_sample(q_lat, q_rope, cache_c, cache_kr_t, aug, layer, row0, streams, t):
    heads, _, kv_lora = q_lat.shape
    past = cache_c.shape[2]
    rope = cache_kr_t.shape[2]
    assert past % CHUNK == 0 and t <= CHUNK and LANES % t == 0 and row0 % LANES == 0
    per_blk = LANES // t
    return pl.pallas_call(
        functools.partial(_mla_attn_sample_kernel, heads=heads, t=t, kv_lora=kv_lora, per_blk=per_blk),
        out_shape=jax.ShapeDtypeStruct((heads, streams * t, kv_lora), BF),
        grid=(streams,),
        in_specs=[
            pl.BlockSpec((heads, t, kv_lora), lambda b: (0, b, 0)),
            pl.BlockSpec((heads, t, LANES), lambda b: (0, b, 0)),
            pl.BlockSpec((None, None, past, kv_lora), lambda b: (layer, b, 0, 0)),
            pl.BlockSpec((None, None, rope, past), lambda b: (layer, b, 0, 0)),
            pl.BlockSpec((LANES, kv_lora + LANES), lambda b: (row0 // LANES + b // per_blk, 0)),
        ],
        out_specs=pl.BlockSpec((heads, t, kv_lora), lambda b: (0, b, 0)),
        compiler_params=_cparams("parallel"),
        name="mla_attn_sample",
    )(q_lat, q_rope, cache_c, cache_kr_t, aug)


def _mla_oup_kernel(ol_ref, w_ref, dst_ref, o_ref):
    del dst_ref
    o_ref[...] = _dot(ol_ref[...], w_ref[...]).astype(BF)


def _mla_oup(o_lat, wuv_t, dst, row0):
    heads, rows, kv_lora = o_lat.shape
    dv = wuv_t.shape[2]
    assert row0 % rows == 0
    return pl.pallas_call(
        _mla_oup_kernel,
        out_shape=jax.ShapeDtypeStruct(dst.shape, dst.dtype),
        grid=(heads,),
        in_specs=[pl.BlockSpec((None, rows, kv_lora), lambda h: (h, 0, 0)),
                  pl.BlockSpec((None, kv_lora, dv), lambda h: (h, 0, 0)),
                  pl.BlockSpec(memory_space=pl.ANY)],
        out_specs=pl.BlockSpec((rows, dv), lambda h: (row0 // rows, h)),
        input_output_aliases={2: 0},
        compiler_params=_cparams("parallel"),
        name="mla_oup",
    )(o_lat, wuv_t, dst)


def _dsa_small_kernel(x_ref, g_ref, w_ref, gk_ref, o_ref, *, idx_dim, wi_scale):
    h = _rms(x_ref[...], g_ref[...]).astype(BF)
    y = _dot_nt(h, w_ref[...].astype(BF))
    lo = y[:, :LANES]
    lane = lax.broadcasted_iota(jnp.int32, lo.shape, 1)
    ms = jnp.sum(jnp.where(lane < idx_dim, lo * lo, 0.0), axis=-1, keepdims=True) / idx_dim
    o_ref[:, :LANES] = lo * lax.rsqrt(ms + NORM_EPS) * gk_ref[...]
    o_ref[:, LANES:] = y[:, LANES:] * wi_scale


def _dsa_small(x, g3, gi, w_small, gk2, tm, idx_dim, wi_scale):
    m, d = x.shape
    return pl.pallas_call(
        functools.partial(_dsa_small_kernel, idx_dim=idx_dim, wi_scale=wi_scale),
        out_shape=jax.ShapeDtypeStruct((m, 2 * LANES), F32),
        grid=(m // tm,),
        in_specs=[pl.BlockSpec((tm, d), lambda i: (i, 0)),
                  pl.BlockSpec((None, 1, d), lambda i: (gi, 0, 0)),
                  pl.BlockSpec((2 * LANES, d), lambda i: (0, 0)),
                  pl.BlockSpec((1, LANES), lambda i: (0, 0))],
        out_specs=pl.BlockSpec((tm, 2 * LANES), lambda i: (i, 0)),
        compiler_params=_cparams("parallel"),
        name="dsa_small",
    )(x, g3, w_small, gk2)


def _bias_tab_kernel(rb_ref, o_ref, *, keys_on_lanes):
    h = pl.program_id(0)
    half = REL_BUCKETS // 2
    max_exact = half // 2
    if keys_on_lanes:
        shape = (o_ref.shape[1], LANES)
        q_axis, k_axis = 0, 1
    else:
        shape = (LANES, o_ref.shape[2])
        q_axis, k_axis = 1, 0
    qi = lax.broadcasted_iota(jnp.int32, shape, q_axis)
    kj = lax.broadcasted_iota(jnp.int32, shape, k_axis)
    far = rb_ref[half - 1, h]
    for v in range(o_ref.shape[0]):
        for side_idx, off in enumerate((v, v - 1)):
            dist = off * LANES + qi - kj
            side = jnp.where(dist < 0, half, 0)
            a = jnp.abs(dist)
            a_f = jnp.maximum(a, 1).astype(F32)
            large = max_exact + (jnp.log(a_f / max_exact) / math.log(REL_MAX_DIST / max_exact)
                                 * (half - max_exact)).astype(jnp.int32)
            large = jnp.minimum(large, half - 1)
            bucket = side + jnp.where(a < max_exact, a, large)
            bias = jnp.zeros(shape, F32)
            for bk in range(REL_BUCKETS):
                bias = jnp.where(bucket == bk, rb_ref[bk, h], bias)
            bias = (bias - far) * LOG2E
            if keys_on_lanes:
                o_ref[v, :, side_idx * LANES:(side_idx + 1) * LANES] = bias
            else:
                o_ref[v, side_idx * LANES:(side_idx + 1) * LANES, :] = bias


def _bias_tab(rel_bias, rows):
    heads = rel_bias.shape[1]
    return pl.pallas_call(
        functools.partial(_bias_tab_kernel, keys_on_lanes=True),
        out_shape=jax.ShapeDtypeStruct((3, heads, rows, KEY_TILE), F32),
        grid=(heads,),
        in_specs=[pl.BlockSpec(memory_space=pltpu.SMEM)],
        out_specs=pl.BlockSpec((3, None, rows, KEY_TILE), lambda h: (0, h, 0, 0)),
        compiler_params=_cparams("parallel"),
        name="bias_tab",
    )(rel_bias)


def _bias_tab_t(rel_bias, kv_heads, tq):
    heads = rel_bias.shape[1]
    group = heads // kv_heads
    return pl.pallas_call(
        functools.partial(_bias_tab_kernel, keys_on_lanes=False),
        out_shape=jax.ShapeDtypeStruct((3, kv_heads, KEY_TILE, group * tq), F32),
        grid=(heads,),
        in_specs=[pl.BlockSpec(memory_space=pltpu.SMEM)],
        out_specs=pl.BlockSpec((3, None, KEY_TILE, tq), lambda h: (0, h // group, 0, h % group)),
        compiler_params=_cparams("parallel"),
        name="bias_tab_t",
    )(rel_bias)


def _topk_threshold(count_ge, shape, topk):
    thr = jnp.where(count_ge(jnp.zeros(shape, jnp.int32)) >= topk, 0, INT_MIN).astype(jnp.int32)

    def bit_body(b, thr):
        cand = thr | (jnp.int32(1) << (30 - b))
        return jnp.where(count_ge(cand) >= topk, cand, thr)

    thr = lax.fori_loop(0, 31, bit_body, thr)
    return jnp.maximum(thr, INT_MIN + 1)


def _dsa_prompt_kernel(q_ref, qi_ref, wi_ref, k_ref, vt_ref, kd_ref, tab_ref, dst_ref, o_ref,
                       key_ref, mb_ref, qim_ref, qg_ref, m_ref, acc_ref, *, tq, topk, kv_heads, group, idx_dim):
    i = pl.program_id(1)
    qb = i * (tq // LANES)
    n_valid = qb // 2 + 1
    n_far = jnp.maximum((qb + 1) // 2 - 1, 0)
    qchunk = (i * tq + lax.broadcasted_iota(jnp.int32, (1, tq), 1)) >> CHUNK_SHIFT
    s_real = k_ref.shape[0]
    dh = LANES
    n_pairs = qim_ref.shape[0]

    lane = lax.broadcasted_iota(jnp.int32, (tq, LANES), 1)
    for j in range(n_pairs):
        pair = qi_ref[:, j * LANES:(j + 1) * LANES].astype(F32)
        qim_ref[j, :tq] = jnp.where(lane < idx_dim, pair, 0.0).astype(BF)
        qim_ref[j, tq:] = jnp.where(lane >= idx_dim, pair, 0.0).astype(BF)
    for g in range(kv_heads):
        qg_ref[g] = jnp.concatenate(
            [q_ref[:, (group * g + r) * dh:(group * g + r + 1) * dh] for r in range(group)], axis=0)
    wi_t = wi_ref[...].T

    def tile_start(t):
        return pl.multiple_of(t * KEY_TILE, KEY_TILE)

    def idx_body(t, carry):
        ks = tile_start(t)
        kd = kd_ref[pl.ds(ks, KEY_TILE), :].astype(BF)
        acc = jnp.zeros((KEY_TILE, tq), F32)
        sc_next = _dot_nt(kd, qim_ref[0])
        for j in range(n_pairs):
            sc = sc_next
            if j + 1 < n_pairs:
                sc_next = _dot_nt(kd, qim_ref[j + 1])
            acc = acc + jnp.maximum(sc[:, :tq], 0.0) * wi_t[2 * j:2 * j + 1, :]
            acc = acc + jnp.maximum(sc[:, tq:], 0.0) * wi_t[2 * j + 1:2 * j + 2, :]
        kpos = ks + lax.broadcasted_iota(jnp.int32, (KEY_TILE, 1), 0)
        valid = ((kpos >> CHUNK_SHIFT) <= qchunk) & (kpos < s_real)
        bits = lax.bitcast_convert_type(acc, jnp.int32)
        key = bits ^ ((bits >> 31) & 0x7FFFFFFF)
        key_ref[pl.ds(ks, KEY_TILE), :] = jnp.where(valid, key, INT_MIN)
        return carry

    lax.fori_loop(0, n_valid, idx_body, 0)

    fold = 64

    def count_ge(cand):
        def body(t, acc):
            c = jnp.where(key_ref[pl.ds(tile_start(t), KEY_TILE), :] >= cand, 1.0, 0.0)
            return acc + jnp.sum(c.reshape(KEY_TILE // fold, fold, tq), axis=0)
        acc = lax.fori_loop(0, n_valid, body, jnp.zeros((fold, tq), F32))
        return jnp.sum(acc, axis=0, keepdims=True)

    thr = _topk_threshold(count_ge, (1, tq), topk)

    def mb_body(t, carry):
        ks = tile_start(t)
        mb_ref[pl.ds(ks, KEY_TILE), :] = jnp.where(key_ref[pl.ds(ks, KEY_TILE), :] >= thr, 0.0, NEG)
        return carry

    lax.fori_loop(0, n_valid, mb_body, 0)

    m_ref[...] = jnp.full(m_ref.shape, NEG, F32)
    acc_ref[...] = jnp.zeros(acc_ref.shape, F32)
    dva = acc_ref.shape[1]

    def tile(t, near):
        ks = tile_start(t)
        mb = mb_ref[pl.ds(ks, KEY_TILE), :]
        mb = jnp.concatenate([mb] * group, axis=1)

        def scores(g):
            s_t = _dot_nt(k_ref[pl.ds(ks, KEY_TILE), g * dh:(g + 1) * dh], qg_ref[g]) + mb
            return s_t + tab_ref[qb - 2 * t, g] if near else s_t

        s_next = scores(0)
        for g in range(kv_heads):
            s_t = s_next
            if g + 1 < kv_heads:
                s_next = scores(g + 1)
            _online_update_t(s_t, vt_ref[g * dva:(g + 1) * dva, pl.ds(ks, KEY_TILE)], m_ref, None, acc_ref, g)

    def far_pair(u, carry):
        tile(2 * u, False)
        tile(2 * u + 1, False)
        return carry

    def far_body(t, carry):
        tile(t, False)
        return carry

    def near_body(t, carry):
        tile(t, True)
        return carry

    lax.fori_loop(0, n_far // 2, far_pair, 0)
    lax.fori_loop(n_far // 2 * 2, n_far, far_body, 0)
    lax.fori_loop(n_far, n_valid, near_body, 0)
    for g in range(kv_heads):
        o_t = acc_ref[g, :dh] / acc_ref[g, dh:dh + 1]
        for r in range(group):
            h = group * g + r
            o_ref[:, h * dh:(h + 1) * dh] = o_t[:, r * tq:(r + 1) * tq].T.astype(BF)


def _dsa_sample_kernel(q_ref, qi_ref, wi_ref, kdc_ref, kdn_ref, kn_ref, vn_ref, tab_ref, kc_ref, vc_ref, dst_ref,
                       o_ref, key_ref, qim_ref, *, t, past, topk, kv_heads, group, idx_dim):
    del dst_ref
    dh = LANES
    idx_heads = qim_ref.shape[0] // t

    def cached(ref, g, c):
        return ref[pl.ds(c * KEY_TILE * kv_heads + g, KEY_TILE, stride=kv_heads), :].astype(BF)

    qb = past // LANES
    n_c = past // KEY_TILE
    s_real = past + t
    qchunk = (past + lax.broadcasted_iota(jnp.int32, (t, 1), 0)) >> CHUNK_SHIFT

    lane = lax.broadcasted_iota(jnp.int32, (t, LANES), 1)
    per_pair = LANES // idx_dim
    wi = wi_ref[...]
    w_rows = []
    for h in range(idx_heads):
        pair = qi_ref[:, (h // per_pair) * LANES:(h // per_pair + 1) * LANES].astype(F32)
        sub = h % per_pair
        keep = (lane >= sub * idx_dim) & (lane < (sub + 1) * idx_dim)
        qim_ref[h * t:(h + 1) * t, :] = jnp.where(keep, pair, 0.0).astype(BF)
        w_rows.append(jnp.broadcast_to(wi[:, h:h + 1], (t, LANES)))
    w_rows = jnp.concatenate(w_rows, axis=0)

    def keys_of(sc, kpos):
        w = sc.shape[1]
        sc = jnp.maximum(sc, 0.0) * jnp.concatenate([w_rows] * (w // LANES), axis=1)
        acc = jnp.sum(sc.reshape(idx_heads, t, w), axis=0)
        valid = ((kpos >> CHUNK_SHIFT) <= qchunk) & (kpos < s_real)
        bits = lax.bitcast_convert_type(acc, jnp.int32)
        key = bits ^ ((bits >> 31) & 0x7FFFFFFF)
        return jnp.where(valid, key, INT_MIN)

    ki_t = kdc_ref[...]
    kd_t = jnp.concatenate([ki_t] * per_pair, axis=0).astype(BF)
    for c in range(n_c):
        kpos = c * KEY_TILE + lax.broadcasted_iota(jnp.int32, (t, KEY_TILE), 1)
        sc = _dot(qim_ref[...], kd_t[:, c * KEY_TILE:(c + 1) * KEY_TILE])
        key_ref[:, c * KEY_TILE:(c + 1) * KEY_TILE] = keys_of(sc, kpos)
    kpos = past + lax.broadcasted_iota(jnp.int32, (t, LANES), 1)
    key_ref[:, past:] = keys_of(_dot_nt(qim_ref[...], kdn_ref[...].astype(BF)), kpos)

    def count_ge(cand):
        return jnp.sum(jnp.where(key_ref[...] >= cand, 1.0, 0.0), axis=-1, keepdims=True)

    thr = _topk_threshold(count_ge, (t, 1), topk)
    mb = jnp.where(key_ref[...] >= thr, 0.0, NEG)

    for g in range(kv_heads):
        hs = slice(group * g, group * (g + 1))
        qg = jnp.concatenate(
            [q_ref[:, (group * g + r) * dh:(group * g + r + 1) * dh] for r in range(group)], axis=0)
        parts = []
        for c in range(n_c):
            s = _dot_nt(qg, cached(kc_ref, g, c)).reshape(group, t, KEY_TILE)
            v = qb - 2 * c
            if v <= 2:
                s = s + tab_ref[v, hs]
            parts.append(s)
        s = _dot_nt(qg, kn_ref[:, g * dh:(g + 1) * dh]).reshape(group, t, LANES)
        parts.append(s + tab_ref[0, hs, :, :LANES])
        s_all = jnp.concatenate(parts, axis=-1) + mb[None]
        m = jnp.max(s_all, axis=-1, keepdims=True)
        p = jnp.exp2(s_all - m)
        l = jnp.sum(p, axis=-1, keepdims=True)
        pb = p.astype(BF).reshape(group * t, past + LANES)
        acc = _dot(pb[:, past:], vn_ref[:, g * dh:(g + 1) * dh])
        for c in range(n_c):
            acc = acc + _dot(pb[:, c * KEY_TILE:(c + 1) * KEY_TILE], cached(vc_ref, g, c))
        out = acc.reshape(group, t, dh) / l
        for r in range(group):
            o_ref[:, (group * g + r) * dh:(group * g + r + 1) * dh] = out[r].astype(BF)


def _dsa_prompt(qkv, small, vt, tab_t, dst, *, batch, seq, heads, kv_heads, idx_heads, idx_dim):
    tq = LANES
    assert seq % KEY_TILE == 0
    nt = seq // tq
    dh = LANES
    group = heads // kv_heads
    qcols, kvcols, qicols = heads * dh, kv_heads * dh, idx_heads * idx_dim
    kern = functools.partial(_dsa_prompt_kernel, tq=tq, topk=min(TOPK_MAX, seq // 4),
                             kv_heads=kv_heads, group=group, idx_dim=idx_dim)
    return pl.pallas_call(
        kern,
        out_shape=jax.ShapeDtypeStruct(dst.shape, dst.dtype),
        grid=(batch, nt),
        in_specs=[
            pl.BlockSpec((tq, qcols), lambda b, i: (b * nt + i, 0)),
            pl.BlockSpec((tq, qicols), lambda b, i: (b * nt + i, (qcols + 2 * kvcols) // qicols)),
            pl.BlockSpec((tq, LANES), lambda b, i: (b * nt + i, 1)),
            pl.BlockSpec((seq, kvcols), lambda b, i: (b, qcols // kvcols)),
            pl.BlockSpec((vt.shape[0], seq), lambda b, i: (0, b)),
            pl.BlockSpec((seq, LANES), lambda b, i: (b, 0)),
            pl.BlockSpec((3, kv_heads, KEY_TILE, group * tq), lambda b, i: (0, 0, 0, 0)),
            pl.BlockSpec(memory_space=pl.ANY),
        ],
        input_output_aliases={7: 0},
        out_specs=pl.BlockSpec((tq, qcols), lambda b, i: (b * nt + i, 0)),
        scratch_shapes=[pltpu.VMEM((seq, tq), jnp.int32),
                        pltpu.VMEM((seq, tq), F32),
                        pltpu.VMEM((idx_heads * idx_dim // LANES, 2 * tq, LANES), BF),
                        pltpu.VMEM((kv_heads, group * tq, dh), BF),
                        pltpu.VMEM((kv_heads, 1, group * tq), F32),
                        pltpu.VMEM((kv_heads, vt.shape[0] // kv_heads, group * tq), F32)],
        compiler_params=_cparams("parallel", "arbitrary"),
        name="dsa_prompt",
    )(qkv, qkv, small, qkv, vt, small, tab_t, dst)


def _dsa_sample(qkv, small, kd_cache, kd_new, k_new, v_new, cache_k, cache_v, tab, dst, *, layer, row0, streams, t,
                heads, kv_heads, idx_heads, idx_dim):
    n_layers, _, past = cache_k.shape[:3]
    dh = LANES
    group = heads // kv_heads
    qcols, kvcols, qicols = heads * dh, kv_heads * dh, idx_heads * idx_dim
    assert past % KEY_TILE == 0 and t <= LANES and row0 % t == 0
    rb0 = row0 // t
    kern = functools.partial(_dsa_sample_kernel, t=t, past=past, topk=min(TOPK_MAX, (past + t) // 4),
                             kv_heads=kv_heads, group=group, idx_dim=idx_dim)
    rows = lambda c: c.reshape(n_layers, streams, past * kv_heads, dh)
    cache_spec = pl.BlockSpec((None, None, past * kv_heads, dh), lambda b: (layer, b, 0, 0))

    return pl.pallas_call(
        kern,
        out_shape=jax.ShapeDtypeStruct(dst.shape, dst.dtype),
        grid=(streams,),
        in_specs=[
            pl.BlockSpec((t, qcols), lambda b: (rb0 + b, 0)),
            pl.BlockSpec((t, qicols), lambda b: (rb0 + b, (qcols + 2 * kvcols) // qicols)),
            pl.BlockSpec((t, LANES), lambda b: (rb0 + b, 1)),
            pl.BlockSpec((None, None, idx_dim, past), lambda b: (layer, b, 0, 0)),
            pl.BlockSpec((LANES, LANES), lambda b: (b, 0)),
            pl.BlockSpec((LANES, kvcols), lambda b: (b, 0)),
            pl.BlockSpec((LANES, kvcols), lambda b: (b, 0)),
            pl.BlockSpec((3, heads, t, KEY_TILE), lambda b: (0, 0, 0, 0)),
            cache_spec,
            cache_spec,
            pl.BlockSpec(memory_space=pl.ANY),
        ],
        out_specs=pl.BlockSpec((t, qcols), lambda b: (rb0 + b, 0)),
        input_output_aliases={10: 0},
        scratch_shapes=[pltpu.VMEM((t, past + LANES), jnp.int32),
                        pltpu.VMEM((idx_heads * t, LANES), BF)],
        compiler_params=_cparams("parallel"),
        name="dsa_sample",
    )(qkv, qkv, small, kd_cache, kd_new, k_new, v_new, tab, rows(cache_k), rows(cache_v), dst)


def _rope_tables(pos, rope):
    half = rope // 2
    inv_freq = ROPE_THETA ** (-jnp.arange(half, dtype=F32) / half)
    ang = pos.astype(F32)[:, None] * inv_freq[None, :]
    cos, sin = jnp.cos(ang), jnp.sin(ang)
    pad = jnp.zeros((pos.shape[0], LANES - rope), F32)
    return (jnp.concatenate([cos, cos, pad], axis=1), jnp.concatenate([sin, sin, pad], axis=1))


@jax.jit
def _step(x_prompt, x_sample, cache_mla_latent, cache_mla_krope, cache_dsa_k, cache_dsa_v, cache_dsa_kidx,
          norm_g, norm_f, ffn_w_in, ffn_w_out, mla_w_down, mla_g_q, mla_g_kv, mla_w_uq, mla_w_uk, mla_w_uv,
          mla_w_o, dsa_w_in, dsa_g_kidx, dsa_w_o, rel_bias):
    batch, seq, d_model = x_prompt.shape
    streams, dec, _ = x_sample.shape
    depth = norm_g.shape[0]
    past = cache_mla_latent.shape[2]
    kv_lora = cache_mla_latent.shape[3]
    rope = cache_mla_krope.shape[3]
    q_lora = mla_w_uq.shape[1]
    mla_heads = mla_w_uq.shape[2]
    nope = mla_w_uk.shape[3]
    kv_heads, dh = cache_dsa_k.shape[3], cache_dsa_k.shape[4]
    idx_dim = cache_dsa_kidx.shape[3]
    dsa_heads = dsa_w_o.shape[1] // dh
    idx_heads = rel_bias.shape[1]
    assert nope == LANES and dh == LANES and 2 * rope == LANES and mla_w_uv.shape[3] == LANES
    assert 2 * idx_dim == LANES and idx_heads <= LANES
    mp, ms = batch * seq, streams * dec
    m = mp + ms
    tm = m // 8
    assert tm * 8 == m and tm % 16 == 0
    qcols, kvcols, qicols = dsa_heads * dh, kv_heads * dh, idx_heads * idx_dim
    big = qcols + 2 * kvcols + qicols

    x = jnp.concatenate([x_prompt.reshape(mp, d_model), x_sample.reshape(ms, d_model)], axis=0)
    g3 = norm_g.reshape(depth * 3, 1, d_model)
    pos = jnp.concatenate([jnp.tile(jnp.arange(seq, dtype=jnp.int32), batch),
                           jnp.tile(past + jnp.arange(dec, dtype=jnp.int32), streams)])
    cos_t, sin_t = _rope_tables(pos, rope)
    wd_t = jnp.transpose(mla_w_down, (0, 2, 1))
    dw_t = jnp.transpose(dsa_w_in, (0, 2, 1))
    wq_t = jnp.transpose(mla_w_uq, (0, 2, 3, 1))
    krope_t = jnp.transpose(cache_mla_krope, (0, 1, 3, 2))
    kidx_t = jnp.transpose(cache_dsa_kidx, (0, 1, 3, 2))
    o = jnp.zeros((m, d_model), BF)
    tab = _bias_tab(rel_bias, dec)
    tab_t = _bias_tab_t(rel_bias, kv_heads, LANES)

    def ffn(x, li, s):
        act = _ffn_in(x, g3, li * 3 + 2 * s, ffn_w_in, (li, s), tm, 512)
        return _mm_res(act, ffn_w_out, (li, s), x, 0.5, tm, 256)

    def new_slab(a):
        a = a.reshape(streams, dec, a.shape[-1])
        return jnp.pad(a, ((0, 0), (0, LANES - dec), (0, 0))).reshape(streams * LANES, a.shape[-1])

    new_c, new_kr, new_k, new_v, new_ki = [], [], [], [], []
    for li in range(depth):
        j = li // 2
        gi = li * 3 + 1
        x = ffn(x, li, 0)
        if li % 2 == 0:
            w_kr_t = jnp.pad(wd_t[j, q_lora + kv_lora:], ((0, LANES - rope), (0, 0)))[None]
            d = _norm_mm(x, g3, gi, wd_t, (j,), 0, (q_lora + kv_lora) // 1024, 1024, F32, tm)
            kr2 = _norm_mm(x, g3, gi, w_kr_t, (0,), 0, 1, LANES, F32, tm)
            cq, ckv, aug, kr = _mla_post(d, kr2, mla_g_q[j], mla_g_kv[j], cos_t, sin_t, tm // 2,
                                         q_lora, kv_lora, rope)
            new_c.append(ckv)
            new_kr.append(kr)
            scale = (nope + rope) ** -0.5 * LOG2E
            q_arr = _mla_qup(cq, wq_t, j, cos_t, sin_t, tm, rope, scale, 4)
            w_kexp = jnp.zeros((kv_lora + LANES, mla_heads, 2 * LANES), F32)
            w_kexp = w_kexp.at[:kv_lora, :, :nope].set(mla_w_uk[j])
            w_kexp = w_kexp.at[kv_lora:kv_lora + rope, :, nope:nope + rope].set(
                jnp.broadcast_to(jnp.eye(rope, dtype=F32)[:, None, :], (rope, mla_heads, rope)))
            w_kexp = w_kexp.reshape(kv_lora + LANES, mla_heads * 2 * LANES).astype(BF)
            k_arr = _mm(aug, w_kexp, mp, kv_lora + LANES, 1024, 2048, BF)
            wuv_rows = jnp.transpose(mla_w_uv[j], (1, 2, 0)).reshape(mla_heads * LANES, kv_lora).astype(BF)
            vt_arr = _mm_nt(wuv_rows, aug, mp, kv_lora, 512, 1024, BF)
            o = _mla_attn_prompt(q_arr, k_arr, vt_arr, o, batch, seq, mla_heads, 512, 4)
            wuk_t = jnp.transpose(mla_w_uk[j], (1, 0, 2)).astype(BF)
            wuv_t = jnp.transpose(mla_w_uv[j], (1, 0, 2)).astype(BF)
            q_lat, q_rope = _mla_qlat(q_arr, wuk_t, mp, ms)
            o_lat = _mla_attn_sample(q_lat, q_rope, cache_mla_latent, krope_t, aug, j, mp, streams, dec)
            o = _mla_oup(o_lat, wuv_t, o, mp)
            x = _mm_res(o, mla_w_o, (j,), x, 1.0, 2 * tm, 256)
        else:
            assert 2 * kvcols == 1024 and qcols % 1024 == 0
            qkv, kv32 = _norm_mm(x, g3, gi, dw_t, (j,), 0, big // 1024, 1024, BF, tm,
                                 scale_blocks=(qcols // 1024, dh ** -0.5 * LOG2E), keep_blk=qcols // 1024)
            w_ki = dw_t[j, big:big + idx_dim]
            w_wi = dw_t[j, big + idx_dim:]
            w_small = jnp.concatenate([w_ki, w_ki, w_wi, jnp.zeros((LANES - idx_heads, d_model), F32)], axis=0)
            gk2 = jnp.tile(dsa_g_kidx[j], 2).reshape(1, LANES)
            small = _dsa_small(x, g3, gi, w_small, gk2, tm, idx_dim, (idx_dim ** -0.5) * (idx_heads ** -0.5))
            new_k.append(kv32[:, :kvcols])
            new_v.append(kv32[:, kvcols:])
            new_ki.append(small[:, :idx_dim])
            common = dict(heads=dsa_heads, kv_heads=kv_heads, idx_heads=idx_heads, idx_dim=idx_dim)
            vt = qkv[:mp, qcols + kvcols:qcols + 2 * kvcols].T.reshape(kv_heads, dh, mp)
            vt = jnp.concatenate([vt, jnp.ones((kv_heads, 16, mp), BF)], axis=1).reshape(kv_heads * (dh + 16), mp)
            o = _dsa_prompt(qkv, small, vt, tab_t, o, batch=batch, seq=seq, **common)
            o = _dsa_sample(qkv, small, kidx_t, new_slab(small[mp:, :LANES]),
                            new_slab(qkv[mp:, qcols:qcols + kvcols]),
                            new_slab(qkv[mp:, qcols + kvcols:qcols + 2 * kvcols]),
                            cache_dsa_k, cache_dsa_v, tab, o, layer=j, row0=mp, streams=streams, t=dec, **common)
            x = _mm_res(o, dsa_w_o, (j,), x, 1.0, 2 * tm, 256)
        x = ffn(x, li, 1)

    tf = math.gcd(mp, ms)
    y_p = _final_norm(x, norm_f, 0, mp, tf)
    y_s = _final_norm(x, norm_f, mp, ms, tf)

    def split(rows, *tail):
        a = jnp.stack(rows)
        n = a.shape[0]
        return (a[:, :mp].reshape(n, batch, seq, *tail), a[:, mp:].reshape(n, streams, dec, *tail))

    p_c, s_c = split(new_c, kv_lora)
    p_kr, s_kr = split(new_kr, rope)
    p_k, s_k = split(new_k, kv_heads, dh)
    p_v, s_v = split(new_v, kv_heads, dh)
    p_ki, s_ki = split(new_ki, idx_dim)
    return (y_p.reshape(batch, seq, d_model), y_s.reshape(streams, dec, d_model),
            p_c, p_kr, p_k, p_v, p_ki, s_c, s_kr, s_k, s_v, s_ki)


def kernel(x_prompt, x_sample, cache_mla_latent, cache_mla_krope, cache_dsa_k, cache_dsa_v, cache_dsa_kidx,
           norm_g, norm_f, ffn_w_in, ffn_w_out, mla_w_down, mla_g_q, mla_g_kv, mla_w_uq, mla_w_uk, mla_w_uv,
           mla_w_o, dsa_w_in, dsa_g_kidx, dsa_w_o, rel_bias):
    return _step(x_prompt, x_sample, cache_mla_latent, cache_mla_krope, cache_dsa_k, cache_dsa_v,
                 cache_dsa_kidx, norm_g, norm_f, ffn_w_in, ffn_w_out, mla_w_down, mla_g_q, mla_g_kv,
                 mla_w_uq, mla_w_uk, mla_w_uv, mla_w_o, dsa_w_in, dsa_g_kidx, dsa_w_o, rel_bias)
```
